```python
import math
import jax, jax.numpy as jnp
from jax import lax
import numpy as np

D_MODEL = 1024
BATCH = 8
SEQ = 8192
DEPTH = 1
DEC_BATCH = 128
DEC_SEQ = 1
PAST_LEN = 8192
PAGE_SIZE = 128

N_MEM = 256
N_BRANCH = 3
W_BRANCH = D_MODEL // 2
CONV_WIDTH = 3
SB_HEADS = 8
SB_HEAD_DIM = W_BRANCH // SB_HEADS
SB_BIAS_INIT = -7.0
MEM_HEADS = 4
MEM_HEAD_DIM = W_BRANCH // MEM_HEADS
D_FF = ((8 * D_MODEL // 3 + 127) // 128) * 128
Q_BLOCK = 128
EPS = 1e-6
IN_WIDTH = 7 * W_BRANCH + N_BRANCH * D_MODEL

kernel_name = 'hybrid_shortconv_stickbreaking_memattn_convffn_step'


def rms_norm(x, g):
    xf = x.astype(jnp.float32)
    y = xf * lax.rsqrt(jnp.mean(xf * xf, axis=-1, keepdims=True) + EPS)
    return (y * g.astype(jnp.float32)).astype(x.dtype)


def causal_dwconv(u, prev, w, b):
    t = u.shape[1]
    full = jnp.concatenate([prev.astype(u.dtype), u], axis=1)
    y = b + w[CONV_WIDTH - 1] * full[:, CONV_WIDTH - 1:]
    for i in range(CONV_WIDTH - 1):
        y = y + w[i] * full[:, i:i + t]
    return y, full[:, t:]


def sb_logits(q, k, bias):
    z = jnp.einsum('bqhd,bkhd->bhqk', q, k).astype(jnp.float32) * (SB_HEAD_DIM ** -0.5)
    return z + bias.astype(jnp.float32)[None, :, None, None]


def sb_weights(z, causal):
    log_beta = jax.nn.log_sigmoid(z)
    log_rest = jnp.where(causal, jax.nn.log_sigmoid(-z), 0.0)
    log_after = lax.cumsum(log_rest, axis=z.ndim - 1, reverse=True) - log_rest
    return jnp.where(causal, jnp.exp(log_beta + log_after), 0.0)


def sb_attend_prompt(q, k, v, bias):
    b, t, h, d = q.shape
    nb = t // Q_BLOCK
    pos = jnp.arange(t, dtype=jnp.int32)
    q_blocks = q.reshape(b, nb, Q_BLOCK, h, d).swapaxes(0, 1)
    p_blocks = pos.reshape(nb, Q_BLOCK)

    def one_block(args):
        qb, pb = args
        a = sb_weights(sb_logits(qb, k, bias), pos[None, :] < pb[:, None])
        return jnp.einsum('bhqk,bkhd->bqhd', a.astype(v.dtype), v)

    o = lax.map(one_block, (q_blocks, p_blocks))
    return o.swapaxes(0, 1).reshape(b, t, h, d)


def sb_attend_sample(q, k, v, bias, past_k, past_v):
    t = q.shape[1]
    past = past_k.shape[1]
    q_pos = past + jnp.arange(t, dtype=jnp.int32)
    k_pos = jnp.arange(past + t, dtype=jnp.int32)
    z = jnp.concatenate([sb_logits(q, past_k.astype(q.dtype), bias), sb_logits(q, k, bias)], axis=-1)
    a = sb_weights(z, k_pos[None, :] < q_pos[:, None]).astype(v.dtype)
    return (jnp.einsum('bhqk,bkhd->bqhd', a[..., :past], past_v.astype(v.dtype))
            + jnp.einsum('bhqk,bkhd->bqhd', a[..., past:], v))


def memory_kv(mem, g_mem, w_mem_kv):
    kv = rms_norm(mem, g_mem) @ w_mem_kv
    mk, mv = jnp.split(kv, 2, axis=-1)
    b = mem.shape[0]
    return (mk.reshape(b, N_MEM, MEM_HEADS, MEM_HEAD_DIM), mv.reshape(b, N_MEM, MEM_HEADS, MEM_HEAD_DIM))


def memory_attend(q, mk, mv):
    s = jnp.einsum('bthd,bmhd->bhtm', q, mk).astype(jnp.float32) * (MEM_HEAD_DIM ** -0.5)
    p = jax.nn.softmax(s, axis=-1)
    return jnp.einsum('bhtm,bmhd->bthd', p.astype(mv.dtype), mv)


def layer(x, conv_prev, ffn_prev, mem_k, mem_v, attend, p):
    b, t, _ = x.shape
    W = W_BRANCH
    h = rms_norm(x, p['g_mix_pre'])
    proj = h @ p['w_in']
    cb, cc, cu, q, k, v, mq, g = jnp.split(proj, [W, 2 * W, 3 * W, 4 * W, 5 * W, 6 * W, 7 * W], axis=-1)
    conv_out, conv_state = causal_dwconv(cc * cu, conv_prev, p['conv_w'], p['conv_b'])
    y_conv = cb * conv_out
    q = q.reshape(b, t, SB_HEADS, SB_HEAD_DIM)
    k = k.reshape(b, t, SB_HEADS, SB_HEAD_DIM)
    v = v.reshape(b, t, SB_HEADS, SB_HEAD_DIM)
    y_sb = attend(q, k, v, p['sb_bias']).reshape(b, t, W)
    y_mem = memory_attend(mq.reshape(b, t, MEM_HEADS, MEM_HEAD_DIM), mem_k, mem_v).reshape(b, t, W)
    ys = jnp.stack([y_conv, y_sb, y_mem], axis=2)
    branch = jnp.einsum('btnw,nwd->btnd', ys, p['w_branch'])
    gate = jax.nn.sigmoid(g.reshape(b, t, N_BRANCH, D_MODEL))
    mixed = jnp.sum(gate * branch, axis=2) @ p['w_o']
    x = x + rms_norm(mixed, p['g_mix_post'])
    up = rms_norm(x, p['g_ffn_pre']) @ p['w_up']
    up_c, ffn_state = causal_dwconv(up, ffn_prev, p['ffn_conv_w'], p['ffn_conv_b'])
    a, bv = jnp.split(up_c, 2, axis=-1)
    f = (jax.nn.gelu(a, approximate=True) * bv) @ p['w_down']
    x = x + rms_norm(f, p['g_ffn_post'])
    return x, conv_state, k, v, ffn_state


def setup_inputs(seed: int = 0) -> dict:
    key = jax.random.key(seed)
    ks = jax.random.split(key, 32)
    n_pages = PAST_LEN // PAGE_SIZE
    n_pool = (DEC_BATCH * n_pages * 5) // 4
    f32 = jnp.float32

    def nrm(k, shape, s):
        return jax.random.normal(k, shape, f32) * s

    def gain(k):
        return 1.0 + 0.01 * jax.random.normal(k, (DEPTH, D_MODEL), f32)

    page_table = jax.random.permutation(ks[0], n_pool)[:DEC_BATCH * n_pages].reshape(DEC_BATCH, n_pages).astype(jnp.int32)
    return {
        'x_prompt': nrm(ks[1], (BATCH, SEQ, D_MODEL), 1.0),
        'x_sample': nrm(ks[2], (DEC_BATCH, DEC_SEQ, D_MODEL), 1.0),
        'cache_sb_k': nrm(ks[3], (DEPTH, n_pool, PAGE_SIZE, SB_HEADS, SB_HEAD_DIM), 1.0),
        'cache_sb_v': nrm(ks[4], (DEPTH, n_pool, PAGE_SIZE, SB_HEADS, SB_HEAD_DIM), 1.0),
        'cache_mem_k': nrm(ks[5], (DEPTH, DEC_BATCH, N_MEM, MEM_HEADS, MEM_HEAD_DIM), 1.0),
        'cache_mem_v': nrm(ks[6], (DEPTH, DEC_BATCH, N_MEM, MEM_HEADS, MEM_HEAD_DIM), 1.0),
        'state_conv': nrm(ks[7], (DEPTH, DEC_BATCH, CONV_WIDTH - 1, W_BRANCH), 1.0),
        'state_ffn_conv': nrm(ks[8], (DEPTH, DEC_BATCH, CONV_WIDTH - 1, 2 * D_FF), 1.0),
        'page_table': page_table,
        'mem_prompt': nrm(ks[9], (BATCH, N_MEM, D_MODEL), 1.0),
        'g_mix_pre': gain(ks[10]),
        'g_mix_post': gain(ks[11]),
        'g_ffn_pre': gain(ks[12]),
        'g_ffn_post': gain(ks[13]),
        'g_mem': gain(ks[14]),
        'w_in': nrm(ks[15], (DEPTH, D_MODEL, IN_WIDTH), D_MODEL ** -0.5),
        'conv_w': nrm(ks[16], (DEPTH, CONV_WIDTH, W_BRANCH), CONV_WIDTH ** -0.5),
        'conv_b': nrm(ks[17], (DEPTH, W_BRANCH), 0.01),
        'sb_bias': SB_BIAS_INIT + 0.1 * jax.random.normal(ks[25], (DEPTH, SB_HEADS), f32),
        'w_mem_kv': nrm(ks[18], (DEPTH, D_MODEL, 2 * W_BRANCH), D_MODEL ** -0.5),
        'w_branch': nrm(ks[19], (DEPTH, N_BRANCH, W_BRANCH, D_MODEL), W_BRANCH ** -0.5),
        'w_o': nrm(ks[20], (DEPTH, D_MODEL, D_MODEL), D_MODEL ** -0.5),
        'w_up': nrm(ks[21], (DEPTH, D_MODEL, 2 * D_FF), D_MODEL ** -0.5),
        'ffn_conv_w': nrm(ks[22], (DEPTH, CONV_WIDTH, 2 * D_FF), CONV_WIDTH ** -0.5),
        'ffn_conv_b': nrm(ks[23], (DEPTH, 2 * D_FF), 0.01),
        'w_down': nrm(ks[24], (DEPTH, D_FF, D_MODEL), D_FF ** -0.5),
    }


def reference(x_prompt, x_sample, cache_sb_k, cache_sb_v, cache_mem_k, cache_mem_v, state_conv, state_ffn_conv,
              page_table, mem_prompt, g_mix_pre, g_mix_post, g_ffn_pre, g_ffn_post, g_mem, w_in, conv_w, conv_b,
              sb_bias, w_mem_kv, w_branch, w_o, w_up, ffn_conv_w, ffn_conv_b, w_down):
    xp, xs = x_prompt, x_sample
    bp, bs = xp.shape[0], xs.shape[0]
    conv_p, k_p, v_p, mk_p, mv_p, ffn_p = [], [], [], [], [], []
    conv_s, k_s, v_s, ffn_s = [], [], [], []
    for l in range(DEPTH):
        p = {
            'g_mix_pre': g_mix_pre[l], 'g_mix_post': g_mix_post[l],
            'g_ffn_pre': g_ffn_pre[l], 'g_ffn_post': g_ffn_post[l],
            'w_in': w_in[l], 'conv_w': conv_w[l], 'conv_b': conv_b[l], 'sb_bias': sb_bias[l],
            'w_branch': w_branch[l], 'w_o': w_o[l], 'w_up': w_up[l],
            'ffn_conv_w': ffn_conv_w[l], 'ffn_conv_b': ffn_conv_b[l], 'w_down': w_down[l],
        }
        mk, mv = memory_kv(mem_prompt, g_mem[l], w_mem_kv[l])
        zero_conv = jnp.zeros((bp, CONV_WIDTH - 1, W_BRANCH), xp.dtype)
        zero_ffn = jnp.zeros((bp, CONV_WIDTH - 1, 2 * D_FF), xp.dtype)
        xp, cs, kk, vv, fs = layer(xp, zero_conv, zero_ffn, mk, mv, sb_attend_prompt, p)
        conv_p.append(cs); k_p.append(kk); v_p.append(vv); mk_p.append(mk); mv_p.append(mv); ffn_p.append(fs)
        past_k = cache_sb_k[l][page_table].reshape(bs, -1, SB_HEADS, SB_HEAD_DIM)
        past_v = cache_sb_v[l][page_table].reshape(bs, -1, SB_HEADS, SB_HEAD_DIM)
        attend = lambda q, k, v, bias: sb_attend_sample(q, k, v, bias, past_k, past_v)
        xs, cs, kk, vv, fs = layer(xs, state_conv[l], state_ffn_conv[l], cache_mem_k[l], cache_mem_v[l], attend, p)
        conv_s.append(cs); k_s.append(kk); v_s.append(vv); ffn_s.append(fs)
    conv_state_prompt = jnp.stack(conv_p)
    sb_k_prompt = jnp.stack(k_p)
    sb_v_prompt = jnp.stack(v_p)
    mem_k_prompt = jnp.stack(mk_p)
    mem_v_prompt = jnp.stack(mv_p)
    ffn_state_prompt = jnp.stack(ffn_p)
    conv_state_sample = jnp.stack(conv_s)
    sb_k_sample = jnp.stack(k_s)
    sb_v_sample = jnp.stack(v_s)
    ffn_state_sample = jnp.stack(ffn_s)
    return (xp, xs, conv_state_prompt, sb_k_prompt, sb_v_prompt, mem_k_prompt, mem_v_prompt, ffn_state_prompt,
            conv_state_sample, sb_k_sample, sb_v_sample, ffn_state_sample)
```

```python
import functools
import math

import jax
import jax.numpy as jnp
from jax import lax
from jax.experimental import pallas as pl
from jax.experimental.pallas import tpu as pltpu

EPS = 1e-6
CONV_WIDTH = 3
N_BRANCH = 3
F32 = jnp.float32
BF16 = jnp.bfloat16

V7X_LANES = 128
V7X_SUBLANES = 8
V7X_VMEM_BYTES = 64 * 1024 * 1024
VMEM_LIMIT_BYTES = (V7X_VMEM_BYTES * 7) // 8


def _params(*semantics):
    return pltpu.CompilerParams(dimension_semantics=semantics, vmem_limit_bytes=VMEM_LIMIT_BYTES)


def _rms(x, g):
    ms = jnp.mean(x * x, axis=-1, keepdims=True)
    return (x * lax.rsqrt(ms + EPS)) * g


def _dot(a, b):
    return jnp.dot(a, b, preferred_element_type=F32)


def _dot_nt(a, b):
    return lax.dot_general(a, b, (((1,), (1,)), ((), ())), preferred_element_type=F32)


def _const_spec(shape):
    nd = len(shape)
    return pl.BlockSpec(shape, lambda *_: (0,) * nd)


def _pick_tile(n, want):
    t = min(n, want)
    assert n % t == 0, (n, t)
    return t


def _memkv_body(mem_ref, g_ref, w_ref, mk_ref, mv_ref, mkb_ref, mvb_ref):
    h = _rms(mem_ref[0], g_ref[...]).astype(BF16)
    kv = _dot(h, w_ref[...])
    w = kv.shape[1] // 2
    mk_ref[0] = kv[:, :w]
    mv_ref[0] = kv[:, w:]
    mkb_ref[0] = kv[:, :w].astype(BF16)
    mvb_ref[0] = kv[:, w:].astype(BF16)


def _memkv(mem, g_mem, w_mem_kv_bf):
    b, n_mem, d = mem.shape
    w = w_mem_kv_bf.shape[1] // 2
    blk = pl.BlockSpec((1, n_mem, w), lambda i: (i, 0, 0))
    return pl.pallas_call(
        _memkv_body,
        grid=(b,),
        in_specs=[pl.BlockSpec((1, n_mem, d), lambda i: (i, 0, 0)), _const_spec((1, d)),
                  _const_spec((d, 2 * w))],
        out_specs=[blk, blk, blk, blk],
        out_shape=[jax.ShapeDtypeStruct((b, n_mem, w), F32)] * 2
        + [jax.ShapeDtypeStruct((b, n_mem, w), BF16)] * 2,
        compiler_params=_params("arbitrary"),
        name="memkv",
    )(mem, g_mem, w_mem_kv_bf)


def _softmax_rows(s):
    m = jnp.max(s, axis=-1, keepdims=True)
    p = jnp.exp(s - m)
    return p / jnp.sum(p, axis=-1, keepdims=True)


def _inproj_prompt_body(x_ref, g_ref, w_ref, cw_ref, cb_ref, mk_ref, mv_ref,
                        ya_ref, q_ref, k_ref, v_ref, kb_ref, vb_ref, yc_ref, cs_ref,
                        s_scr, *, tm, w, sb_scale, mem_hd):
    ti = pl.program_id(1)
    h = _rms(x_ref[0], g_ref[...]).astype(BF16)

    def grp(i):
        return _dot(h, w_ref[:, i * w:(i + 1) * w])

    ccu = grp(1) * grp(2)
    tail = s_scr[tm:tm + V7X_SUBLANES, :]
    s_scr[0:V7X_SUBLANES, :] = jnp.where(ti == 0, jnp.zeros_like(tail), tail)
    s_scr[V7X_SUBLANES:, :] = ccu
    r1 = s_scr[pl.ds(V7X_SUBLANES - 1, tm), :]
    r2 = s_scr[pl.ds(V7X_SUBLANES - 2, tm), :]
    cw = cw_ref[...]
    conv = cb_ref[...] + cw[2:3] * ccu
    conv = conv + cw[0:1] * r2
    conv = conv + cw[1:2] * r1
    ya_ref[0] = (grp(0) * conv).astype(BF16)
    cs_ref[0] = s_scr[pl.ds(tm + V7X_SUBLANES - (CONV_WIDTH - 1), CONV_WIDTH - 1), :]

    q_ref[0] = (grp(3) * sb_scale).astype(BF16)
    k = grp(4)
    k_ref[0] = k
    kb_ref[0] = k.astype(BF16)
    v = grp(5)
    v_ref[0] = v
    vb_ref[0] = v.astype(BF16)

    mq = grp(6)
    outs = []
    for hd in range(w // mem_hd):
        sl = slice(hd * mem_hd, (hd + 1) * mem_hd)
        s = _dot_nt(mq[:, sl].astype(BF16), mk_ref[0, :, sl]) * (mem_hd ** -0.5)
        p = _softmax_rows(s)
        outs.append(_dot(p.astype(BF16), mv_ref[0, :, sl]))
    yc_ref[0] = jnp.concatenate(outs, axis=1).astype(BF16)


def _inproj_prompt(x, g, w_a, conv_w, conv_b, mkb, mvb, *, sb_scale, mem_hd, tm):
    b, t, d = x.shape
    w = conv_w.shape[1]
    n_mem = mkb.shape[1]
    row = lambda dt: jax.ShapeDtypeStruct((b, t, w), dt)
    blk = pl.BlockSpec((1, tm, w), lambda i, j: (i, j, 0))
    body = functools.partial(_inproj_prompt_body, tm=tm, w=w, sb_scale=sb_scale, mem_hd=mem_hd)
    return pl.pallas_call(
        body,
        grid=(b, t // tm),
        in_specs=[pl.BlockSpec((1, tm, d), lambda i, j: (i, j, 0)), _const_spec((1, d)),
                  _const_spec(w_a.shape), _const_spec(conv_w.shape), _const_spec((1, w)),
                  pl.BlockSpec((1, n_mem, w), lambda i, j: (i, 0, 0)),
                  pl.BlockSpec((1, n_mem, w), lambda i, j: (i, 0, 0))],
        out_specs=[blk] * 7 + [pl.BlockSpec((1, CONV_WIDTH - 1, w), lambda i, j: (i, 0, 0))],
        out_shape=[row(BF16), row(BF16), row(F32), row(F32), row(BF16), row(BF16), row(BF16),
                   jax.ShapeDtypeStruct((b, CONV_WIDTH - 1, w), F32)],
        scratch_shapes=[pltpu.VMEM((tm + V7X_SUBLANES, w), F32)],
        compiler_params=_params("arbitrary", "arbitrary"),
        name="inproj_prompt",
    )(x, g, w_a, conv_w, conv_b, mkb, mvb)


def _inproj_sample_body(x_ref, g_ref, w_ref, cw_ref, cb_ref, st0_ref, st1_ref,
                        ya_ref, q_ref, k_ref, v_ref, mq_ref, ccu_ref, *, w, sb_scale):
    h = _rms(x_ref[...], g_ref[...]).astype(BF16)

    def grp(i):
        return _dot(h, w_ref[:, i * w:(i + 1) * w])

    ccu = grp(1) * grp(2)
    cw = cw_ref[...]
    conv = cb_ref[...] + cw[2:3] * ccu
    conv = conv + cw[0:1] * st0_ref[...]
    conv = conv + cw[1:2] * st1_ref[...]
    ya_ref[...] = (grp(0) * conv).astype(BF16)
    ccu_ref[...] = ccu
    q_ref[...] = grp(3) * sb_scale
    k_ref[...] = grp(4)
    v_ref[...] = grp(5)
    mq_ref[...] = grp(6)


def _inproj_sample(x, g, w_a, conv_w, conv_b, st0, st1, *, sb_scale):
    n, d = x.shape
    w = conv_w.shape[1]
    row = lambda dt: jax.ShapeDtypeStruct((n, w), dt)
    body = functools.partial(_inproj_sample_body, w=w, sb_scale=sb_scale)
    return pl.pallas_call(
        body,
        grid=(1,),
        in_specs=[_const_spec((n, d)), _const_spec((1, d)), _const_spec(w_a.shape),
                  _const_spec(conv_w.shape), _const_spec((1, w)), _const_spec((n, w)),
                  _const_spec((n, w))],
        out_specs=[_const_spec((n, w))] * 6,
        out_shape=[row(BF16), row(F32), row(F32), row(F32), row(F32), row(F32)],
        compiler_params=_params("arbitrary"),
        name="inproj_sample",
    )(x, g, w_a, conv_w, conv_b, st0, st1)


def _sb_block(z, causal, uu, carry):
    l = jnp.log1p(jnp.exp(-jnp.abs(z)))
    log_beta = jnp.minimum(z, 0.0) - l
    log_rest = -(jnp.maximum(z, 0.0) + l)
    if causal is not None:
        log_rest = jnp.where(causal, log_rest, 0.0)
    hi = log_rest.astype(BF16)
    lo = (log_rest - hi.astype(F32)).astype(BF16)
    log_after = _dot(jnp.concatenate([hi, lo], axis=1), uu)
    a = jnp.exp(log_beta + log_after + carry)
    if causal is not None:
        a = jnp.where(causal, a, 0.0)
    return a, jnp.sum(log_rest, axis=-1, keepdims=True)


def _strict_lower(n):
    row = lax.broadcasted_iota(jnp.int32, (n, n), 0)
    col = lax.broadcasted_iota(jnp.int32, (n, n), 1)
    return row > col


def _sb_prompt_body(bias_ref, q_ref, k_ref, v_ref, o_ref, acc_scr, carry_scr, *, tq, hd):
    hp = pl.program_id(1)
    i = pl.program_id(2)
    hpb = V7X_LANES // hd
    q2 = q_ref[0]
    lane_head = lax.broadcasted_iota(jnp.int32, (tq, V7X_LANES), 1) // hd
    causal = _strict_lower(tq)
    u = jnp.where(causal, 1.0, 0.0).astype(BF16)
    uu = jnp.concatenate([u, u], axis=0)

    for h in range(hpb):
        qh = jnp.where(lane_head == h, q2, jnp.zeros_like(q2))
        bias = bias_ref[hp * hpb + h]

        def step(j, mask):
            ks = pl.multiple_of(j * tq, tq)
            z = _dot_nt(qh, k_ref[0, pl.ds(ks, tq), :]) + bias
            a, rest = _sb_block(z, mask, uu, carry_scr[...])
            acc_scr[h] += _dot(a.astype(BF16), v_ref[0, pl.ds(ks, tq), :])
            carry_scr[...] += rest

        acc_scr[h] = jnp.zeros((tq, V7X_LANES), F32)
        carry_scr[...] = jnp.zeros((tq, 1), F32)
        step(i, causal)

        def loop_body(n, c):
            step(i - 1 - n, None)
            return c

        lax.fori_loop(0, i, loop_body, 0)

    out = acc_scr[hpb - 1]
    for h in range(hpb - 2, -1, -1):
        out = jnp.where(lane_head == h, acc_scr[h], out)
    o_ref[0] = out.astype(BF16)


def _sb_prompt(q, k, v, bias, *, hd, tq):
    b, t, w = q.shape
    body = functools.partial(_sb_prompt_body, tq=tq, hd=hd)
    hpb = V7X_LANES // hd
    grid_spec = pltpu.PrefetchScalarGridSpec(
        num_scalar_prefetch=1,
        grid=(b, w // V7X_LANES, t // tq),
        in_specs=[pl.BlockSpec((1, tq, V7X_LANES), lambda bi, hp, i, _: (bi, i, hp)),
                  pl.BlockSpec((1, t, V7X_LANES), lambda bi, hp, i, _: (bi, 0, hp)),
                  pl.BlockSpec((1, t, V7X_LANES), lambda bi, hp, i, _: (bi, 0, hp))],
        out_specs=pl.BlockSpec((1, tq, V7X_LANES), lambda bi, hp, i, _: (bi, i, hp)),
        scratch_shapes=[pltpu.VMEM((hpb, tq, V7X_LANES), F32), pltpu.VMEM((tq, 1), F32)],
    )
    return pl.pallas_call(
        body,
        grid_spec=grid_spec,
        out_shape=jax.ShapeDtypeStruct((b, t, w), BF16),
        compiler_params=_params("arbitrary", "arbitrary", "arbitrary"),
        name="sb_prompt",
    )(bias, q, k, v)


def _sb_sample_body(pt_ref, q_ref, bias_ref, *refs, pp, hd, rows):
    k_refs = refs[:pp]
    v_refs = refs[pp:2 * pp]
    o_ref = refs[2 * pp]
    acc_scr, carry_scr = refs[2 * pp + 1:]
    g = pl.program_id(1)
    w = q_ref.shape[-1]
    page = k_refs[0].shape[1]

    @pl.when(g == 0)
    def _():
        acc_scr[...] = jnp.zeros_like(acc_scr)
        carry_scr[...] = jnp.zeros_like(carry_scr)

    row_head = lax.broadcasted_iota(jnp.int32, (rows, w), 0)
    lane_head = lax.broadcasted_iota(jnp.int32, (rows, w), 1) // hd
    on_diag = row_head == lane_head
    qrow = jnp.broadcast_to(q_ref[0], (rows, w))
    qbd = jnp.where(on_diag, qrow, 0.0).astype(BF16)
    u = jnp.where(_strict_lower(page), 1.0, 0.0).astype(BF16)
    uu = jnp.concatenate([u, u], axis=0)

    for r in range(pp - 1, -1, -1):
        z = _dot_nt(qbd, k_refs[r][0].astype(BF16)) + bias_ref[...]
        a, rest = _sb_block(z, None, uu, carry_scr[...])
        acc_scr[...] += _dot(a.astype(BF16), v_refs[r][0].astype(BF16))
        carry_scr[...] += rest

    @pl.when(g == pl.num_programs(1) - 1)
    def _():
        picked = jnp.where(on_diag, acc_scr[...], 0.0)
        o_ref[0] = jnp.sum(picked, axis=0, keepdims=True).astype(BF16)


def _sb_sample(q, bias_col, cache_k, cache_v, page_table, *, hd, pp, rows):
    n, w = q.shape
    n_pages = page_table.shape[1]
    page = cache_k.shape[1]
    assert n_pages % pp == 0
    ng = n_pages // pp

    def page_spec(r):
        return pl.BlockSpec((1, page, w), lambda bi, g, pt: (pt[bi, (ng - 1 - g) * pp + r], 0, 0))

    body = functools.partial(_sb_sample_body, pp=pp, hd=hd, rows=rows)
    grid_spec = pltpu.PrefetchScalarGridSpec(
        num_scalar_prefetch=1,
        grid=(n, ng),
        in_specs=[pl.BlockSpec((1, 1, w), lambda bi, g, pt: (bi, 0, 0)),
                  pl.BlockSpec((rows, 1), lambda bi, g, pt: (0, 0))]
        + [page_spec(r) for r in range(pp)] * 2,
        out_specs=pl.BlockSpec((1, 1, w), lambda bi, g, pt: (bi, 0, 0)),
        scratch_shapes=[pltpu.VMEM((rows, w), F32), pltpu.VMEM((rows, 1), F32)],
    )
    out = pl.pallas_call(
        body,
        grid_spec=grid_spec,
        out_shape=jax.ShapeDtypeStruct((n, 1, w), BF16),
        compiler_params=_params("arbitrary", "arbitrary"),
        name="sb_sample",
    )(page_table, q.reshape(n, 1, w), bias_col, *([cache_k] * pp), *([cache_v] * pp))
    return out.reshape(n, w)


def _mem_sample_body(mq_ref, mk_ref, mv_ref, o_ref, *, mem_hd):
    q = mq_ref[0]
    prod = mk_ref[0] * q
    v = mv_ref[0]
    w = q.shape[-1]
    outs = []
    for h in range(w // mem_hd):
        sl = slice(h * mem_hd, (h + 1) * mem_hd)
        s = jnp.sum(prod[:, sl], axis=-1, keepdims=True) * (mem_hd ** -0.5)
        m = jnp.max(s, axis=0, keepdims=True)
        p = jnp.exp(s - m)
        p = p / jnp.sum(p, axis=0, keepdims=True)
        outs.append(jnp.sum(p * v[:, sl], axis=0, keepdims=True))
    o_ref[0] = jnp.concatenate(outs, axis=1).astype(BF16)


def _mem_sample(mq, mem_k, mem_v, *, mem_hd):
    n, w = mq.shape
    n_mem = mem_k.shape[1]
    row = pl.BlockSpec((1, 1, w), lambda i: (i, 0, 0))
    mem = pl.BlockSpec((1, n_mem, w), lambda i: (i, 0, 0))
    out = pl.pallas_call(
        functools.partial(_mem_sample_body, mem_hd=mem_hd),
        grid=(n,),
        in_specs=[row, mem, mem],
        out_specs=row,
        out_shape=jax.ShapeDtypeStruct((n, 1, w), BF16),
        compiler_params=_params("arbitrary"),
        name="mem_sample",
    )(mq.reshape(n, 1, w), mem_k, mem_v)
    return out.reshape(n, w)


def _mix_body(x_ref, ya_ref, yb_ref, yc_ref, gpre_ref, gpost_ref, wg_ref, wb_ref, wo_ref, o_ref):
    x = x_ref[0]
    h = _rms(x, gpre_ref[...]).astype(BF16)
    ys = (ya_ref[0], yb_ref[0], yc_ref[0])
    merged = None
    for n in range(N_BRANCH):
        gate = 1.0 / (1.0 + jnp.exp(-_dot(h, wg_ref[n])))
        term = gate * _dot(ys[n], wb_ref[n])
        merged = term if merged is None else merged + term
    mixed = _dot(merged.astype(BF16), wo_ref[...])
    o_ref[0] = x + _rms(mixed, gpost_ref[...])


def _mix(x, ya, yb, yc, g_pre, g_post, w_gate, w_branch, w_o, *, tm):
    b, t, d = x.shape
    w = ya.shape[-1]
    xblk = pl.BlockSpec((1, tm, d), lambda i, j: (i, j, 0))
    yblk = pl.BlockSpec((1, tm, w), lambda i, j: (i, j, 0))
    return pl.pallas_call(
        _mix_body,
        grid=(b, t // tm),
        in_specs=[xblk, yblk, yblk, yblk, _const_spec((1, d)), _const_spec((1, d)),
                  _const_spec(w_gate.shape), _const_spec(w_branch.shape), _const_spec(w_o.shape)],
        out_specs=xblk,
        out_shape=jax.ShapeDtypeStruct((b, t, d), F32),
        compiler_params=_params("arbitrary", "arbitrary"),
        name="mix",
    )(x, ya, yb, yc, g_pre, g_post, w_gate, w_branch, w_o)


def _gelu_tanh(x):
    return x * (0.5 * (1.0 + jnp.tanh((2.0 / jnp.pi) ** 0.5 * (x + 0.044715 * (x * x * x)))))


def _ffn_prompt_body(x_ref, gpre_ref, gpost_ref, wup_ref, fcw_ref, fcb_ref, wdn_ref,
                     o_ref, st_ref, hn_scr, f_scr, carry_scr, s_scr, *, tm, nc):
    ti = pl.program_id(1)

    @pl.when(ti == 0)
    def _():
        carry_scr[...] = jnp.zeros_like(carry_scr)

    x = x_ref[0]
    hn_scr[...] = _rms(x, gpre_ref[...]).astype(BF16)
    f_scr[...] = jnp.zeros_like(f_scr)

    def conv_chunk(idx, slot):
        u = _dot(hn_scr[...], wup_ref[idx])
        s_scr[slot, 0:V7X_SUBLANES, :] = carry_scr[idx]
        s_scr[slot, V7X_SUBLANES:, :] = u
        last = u[tm - V7X_SUBLANES:, :]
        carry_scr[idx] = last
        st_ref[0, idx] = last
        r1 = s_scr[slot, pl.ds(V7X_SUBLANES - 1, tm), :]
        r2 = s_scr[slot, pl.ds(V7X_SUBLANES - 2, tm), :]
        cw = fcw_ref[idx]
        y = fcb_ref[idx] + cw[2:3] * u
        y = y + cw[0:1] * r2
        return y + cw[1:2] * r1

    def chunk(c, carry):
        a = conv_chunk(c, 0)
        bv = conv_chunk(c + nc, 1)
        act = (_gelu_tanh(a) * bv).astype(BF16)
        f_scr[...] += _dot(act, wdn_ref[c])
        return carry

    lax.fori_loop(0, nc, chunk, 0)
    o_ref[0] = x + _rms(f_scr[...], gpost_ref[...])


def _ffn_prompt(x, g_pre, g_post, w_up_c, fcw_c, fcb_c, w_dn_c, *, tm):
    b, t, d = x.shape
    nc2, _, cw = w_up_c.shape
    nc = nc2 // 2
    xblk = pl.BlockSpec((1, tm, d), lambda i, j: (i, j, 0))
    body = functools.partial(_ffn_prompt_body, tm=tm, nc=nc)
    return pl.pallas_call(
        body,
        grid=(b, t // tm),
        in_specs=[xblk, _const_spec((1, d)), _const_spec((1, d)), _const_spec(w_up_c.shape),
                  _const_spec(fcw_c.shape), _const_spec(fcb_c.shape), _const_spec(w_dn_c.shape)],
        out_specs=[xblk, pl.BlockSpec((1, nc2, V7X_SUBLANES, cw), lambda i, j: (i, 0, 0, 0))],
        out_shape=[jax.ShapeDtypeStruct((b, t, d), F32),
                   jax.ShapeDtypeStruct((b, nc2, V7X_SUBLANES, cw), F32)],
        scratch_shapes=[pltpu.VMEM((tm, d), BF16), pltpu.VMEM((tm, d), F32),
                        pltpu.VMEM((nc2, V7X_SUBLANES, cw), F32),
                        pltpu.VMEM((2, tm + V7X_SUBLANES, cw), F32)],
        compiler_params=_params("arbitrary", "arbitrary"),
        name="ffn_prompt",
    )(x, g_pre, g_post, w_up_c, fcw_c, fcb_c, w_dn_c)


def _ffn_sample_body(x_ref, gpre_ref, gpost_ref, wup_ref, fcw_ref, fcb_ref, wdn_ref, st0_ref, st1_ref,
                     o_ref, up_ref, hn_scr, f_scr, *, nc):
    x = x_ref[...]
    hn_scr[...] = _rms(x, gpre_ref[...]).astype(BF16)
    f_scr[...] = jnp.zeros_like(f_scr)

    def conv_chunk(idx):
        u = _dot(hn_scr[...], wup_ref[idx])
        up_ref[idx] = u
        cw = fcw_ref[idx]
        y = fcb_ref[idx] + cw[2:3] * u
        y = y + cw[0:1] * st0_ref[idx]
        return y + cw[1:2] * st1_ref[idx]

    def chunk(c, carry):
        act = (_gelu_tanh(conv_chunk(c)) * conv_chunk(c + nc)).astype(BF16)
        f_scr[...] += _dot(act, wdn_ref[c])
        return carry

    lax.fori_loop(0, nc, chunk, 0)
    o_ref[...] = x + _rms(f_scr[...], gpost_ref[...])


def _ffn_sample(x, g_pre, g_post, w_up_c, fcw_c, fcb_c, w_dn_c, st0_c, st1_c):
    n, d = x.shape
    nc2, _, cw = w_up_c.shape
    body = functools.partial(_ffn_sample_body, nc=nc2 // 2)
    return pl.pallas_call(
        body,
        grid=(1,),
        in_specs=[_const_spec((n, d)), _const_spec((1, d)), _const_spec((1, d)), _const_spec(w_up_c.shape),
                  _const_spec(fcw_c.shape), _const_spec(fcb_c.shape), _const_spec(w_dn_c.shape),
                  _const_spec(st0_c.shape), _const_spec(st1_c.shape)],
        out_specs=[_const_spec((n, d)), _const_spec((nc2, n, cw))],
        out_shape=[jax.ShapeDtypeStruct((n, d), F32), jax.ShapeDtypeStruct((nc2, n, cw), F32)],
        scratch_shapes=[pltpu.VMEM((n, d), BF16), pltpu.VMEM((n, d), F32)],
        compiler_params=_params("arbitrary"),
        name="ffn_sample",
    )(x, g_pre, g_post, w_up_c, fcw_c, fcb_c, w_dn_c, st0_c, st1_c)


FFN_CHUNK = 2 * V7X_LANES
SAMPLE_PAGES_PER_STEP = 8
SAMPLE_HEAD_ROWS = 2 * V7X_SUBLANES


def _chunk_cols(a, cw):
    n = a.shape[-1]
    a = a.reshape(a.shape[:-1] + (n // cw, cw))
    return jnp.moveaxis(a, -2, 0)


def _unchunk_cols(a):
    nc, r, cw = a.shape
    return jnp.moveaxis(a, 0, 1).reshape(r, nc * cw)


def kernel(x_prompt, x_sample, cache_sb_k, cache_sb_v, cache_mem_k, cache_mem_v, state_conv, state_ffn_conv,
           page_table, mem_prompt, g_mix_pre, g_mix_post, g_ffn_pre, g_ffn_post, g_mem, w_in, conv_w, conv_b,
           sb_bias, w_mem_kv, w_branch, w_o, w_up, ffn_conv_w, ffn_conv_b, w_down):
    depth = w_in.shape[0]
    bp, seq, d = x_prompt.shape
    bs, dec_seq, _ = x_sample.shape
    assert dec_seq == 1, "the sample group advances one token per step"
    _, n_pool, page, sb_heads, sb_hd = cache_sb_k.shape
    _, _, n_mem, mem_heads, mem_hd = cache_mem_k.shape
    w = sb_heads * sb_hd
    d_ff = w_down.shape[1]
    assert mem_hd % V7X_LANES == 0 and V7X_LANES % sb_hd == 0 and d_ff % FFN_CHUNK == 0
    sb_scale = sb_hd ** -0.5
    assert math.frexp(sb_scale)[0] == 0.5, "the logit scale must be a power of two to fold into q exactly"

    tm_in = _pick_tile(seq, 512)
    tm_mix = _pick_tile(seq, 512)
    tm_ffn = _pick_tile(seq, 256)
    tq = _pick_tile(seq, 256)

    xp = x_prompt
    xs = x_sample.reshape(bs, d)
    conv_p, k_p, v_p, mk_p, mv_p, ffn_p = [], [], [], [], [], []
    conv_s, k_s, v_s, ffn_s = [], [], [], []
    for l in range(depth):
        row = lambda a: a[l].reshape(1, -1)
        w_a = w_in[l][:, :7 * w].astype(BF16)
        w_gate = jnp.moveaxis(w_in[l][:, 7 * w:].reshape(d, N_BRANCH, d), 1, 0).astype(BF16)
        w_br = w_branch[l].astype(BF16)
        w_out = w_o[l].astype(BF16)
        w_up_c = _chunk_cols(w_up[l], FFN_CHUNK).astype(BF16)
        w_dn_c = w_down[l].reshape(d_ff // FFN_CHUNK, FFN_CHUNK, d).astype(BF16)
        fcw_c = _chunk_cols(ffn_conv_w[l], FFN_CHUNK)
        fcb_c = _chunk_cols(ffn_conv_b[l].reshape(1, -1), FFN_CHUNK)
        bias = sb_bias[l]

        mk, mv, mkb, mvb = _memkv(mem_prompt, row(g_mem), w_mem_kv[l].astype(BF16))
        ya, q, k, v, kb, vb, yc, cs = _inproj_prompt(
            xp, row(g_mix_pre), w_a, conv_w[l], row(conv_b), mkb, mvb,
            sb_scale=sb_scale, mem_hd=mem_hd, tm=tm_in)
        yb = _sb_prompt(q, kb, vb, bias, hd=sb_hd, tq=tq)
        x1 = _mix(xp, ya, yb, yc, row(g_mix_pre), row(g_mix_post), w_gate, w_br, w_out, tm=tm_mix)
        xp, st = _ffn_prompt(x1, row(g_ffn_pre), row(g_ffn_post), w_up_c, fcw_c, fcb_c, w_dn_c, tm=tm_ffn)
        fs = jnp.moveaxis(st[:, :, V7X_SUBLANES - (CONV_WIDTH - 1):, :], 1, 2).reshape(bp, CONV_WIDTH - 1, -1)
        conv_p.append(cs)
        k_p.append(k.reshape(bp, seq, sb_heads, sb_hd))
        v_p.append(v.reshape(bp, seq, sb_heads, sb_hd))
        mk_p.append(mk.reshape(bp, n_mem, mem_heads, mem_hd))
        mv_p.append(mv.reshape(bp, n_mem, mem_heads, mem_hd))
        ffn_p.append(fs)

        st_c = state_conv[l]
        ya, q, k, v, mq, ccu = _inproj_sample(xs, row(g_mix_pre), w_a, conv_w[l], row(conv_b),
                                              st_c[:, 0], st_c[:, 1], sb_scale=sb_scale)
        bias_col = jnp.zeros((SAMPLE_HEAD_ROWS, 1), F32).at[:sb_heads, 0].set(bias)
        yb = _sb_sample(q, bias_col, cache_sb_k[l].reshape(n_pool, page, w), cache_sb_v[l].reshape(n_pool, page, w),
                        page_table, hd=sb_hd, pp=SAMPLE_PAGES_PER_STEP, rows=SAMPLE_HEAD_ROWS)
        yc = _mem_sample(mq, cache_mem_k[l].reshape(bs, n_mem, w), cache_mem_v[l].reshape(bs, n_mem, w),
                         mem_hd=mem_hd)
        x1 = _mix(xs[None], ya[None], yb[None], yc[None], row(g_mix_pre), row(g_mix_post),
                  w_gate, w_br, w_out, tm=bs)[0]
        st_f = state_ffn_conv[l]
        xs, up_c = _ffn_sample(x1, row(g_ffn_pre), row(g_ffn_post), w_up_c, fcw_c, fcb_c, w_dn_c,
                               _chunk_cols(st_f[:, 0], FFN_CHUNK), _chunk_cols(st_f[:, 1], FFN_CHUNK))
        conv_s.append(jnp.stack([st_c[:, 1], ccu], axis=1))
        k_s.append(k.reshape(bs, 1, sb_heads, sb_hd))
        v_s.append(v.reshape(bs, 1, sb_heads, sb_hd))
        ffn_s.append(jnp.stack([st_f[:, 1], _unchunk_cols(up_c)], axis=1))

    return (xp, xs.reshape(bs, 1, d), jnp.stack(conv_p), jnp.stack(k_p), jnp.stack(v_p), jnp.stack(mk_p),
            jnp.stack(mv_p), jnp.stack(ffn_p), jnp.stack(conv_s), jnp.stack(k_s), jnp.stack(v_s),
            jnp.stack(ffn_s))
```

```python
import functools
import math

import jax
import jax.numpy as jnp
from jax import lax
from jax.experimental import pallas as pl
from jax.experimental.pallas import tpu as pltpu

EPS = 1e-6
CONV_WIDTH = 3
N_BRANCH = 3
F32 = jnp.float32
BF16 = jnp.bfloat16

V7X_LANES = 128
V7X_SUBLANES = 8
V7X_VMEM_BYTES = 64 * 1024 * 1024
VMEM_LIMIT_BYTES = (V7X_VMEM_BYTES * 7) // 8


def _params(*semantics):
    return pltpu.CompilerParams(dimension_semantics=semantics, vmem_limit_bytes=VMEM_LIMIT_BYTES)


def _rms(x, g):
    ms = jnp.mean(x * x, axis=-1, keepdims=True)
    return (x * lax.rsqrt(ms + EPS)) * g


def _dot(a, b):
    return jnp.dot(a, b, preferred_element_type=F32)


def _dot_nt(a, b):
    return lax.dot_general(a, b, (((1,), (1,)), ((), ())), preferred_element_type=F32)


def _const_spec(shape):
    nd = len(shape)
    return pl.BlockSpec(shape, lambda *_: (0,) * nd)


def _pick_tile(n, want):
    t = min(n, want)
    assert n % t == 0, (n, t)
    return t


def _memkv_body(mem_ref, g_ref, w_ref, mk_ref, mv_ref, mkb_ref, mvb_ref):
    h = _rms(mem_ref[0], g_ref[...]).astype(BF16)
    kv = _dot(h, w_ref[...])
    w = kv.shape[1] // 2
    mk_ref[0] = kv[:, :w]
    mv_ref[0] = kv[:, w:]
    mkb_ref[0] = kv[:, :w].astype(BF16)
    mvb_ref[0] = kv[:, w:].astype(BF16)


def _memkv(mem, g_mem, w_mem_kv_bf):
    b, n_mem, d = mem.shape
    w = w_mem_kv_bf.shape[1] // 2
    blk = pl.BlockSpec((1, n_mem, w), lambda i: (i, 0, 0))
    return pl.pallas_call(
        _memkv_body,
        grid=(b,),
        in_specs=[pl.BlockSpec((1, n_mem, d), lambda i: (i, 0, 0)), _const_spec((1, d)),
                  _const_spec((d, 2 * w))],
        out_specs=[blk, blk, blk, blk],
        out_shape=[jax.ShapeDtypeStruct((b, n_mem, w), F32)] * 2
        + [jax.ShapeDtypeStruct((b, n_mem, w), BF16)] * 2,
        compiler_params=_params("arbitrary"),
        name="memkv",
    )(mem, g_mem, w_mem_kv_bf)


def _softmax_rows(s):
    m = jnp.max(s, axis=-1, keepdims=True)
    p = jnp.exp(s - m)
    return p / jnp.sum(p, axis=-1, keepdims=True)


def _inproj_prompt_body(x_ref, g_ref, w_ref, cw_ref, cb_ref, mk_ref, mv_ref,
                        ya_ref, q_ref, k_ref, v_ref, kb_ref, vb_ref, yc_ref, cs_ref,
                        s_scr, *, tm, w, sb_scale, mem_hd):
    ti = pl.program_id(1)
    h = _rms(x_ref[0], g_ref[...]).astype(BF16)

    def grp(i):
        return _dot(h, w_ref[:, i * w:(i + 1) * w])

    ccu = grp(1) * grp(2)
    @pl.when(ti == 0)
    def _():
        s_scr[tm:tm + V7X_SUBLANES, :] = jnp.zeros((V7X_SUBLANES, w), F32)

    s_scr[0:V7X_SUBLANES, :] = s_scr[tm:tm + V7X_SUBLANES, :]
    s_scr[V7X_SUBLANES:, :] = ccu
    r1 = s_scr[pl.ds(V7X_SUBLANES - 1, tm), :]
    r2 = s_scr[pl.ds(V7X_SUBLANES - 2, tm), :]
    cw = cw_ref[...]
    conv = cb_ref[...] + cw[2:3] * ccu
    conv = conv + cw[0:1] * r2
    conv = conv + cw[1:2] * r1
    ya_ref[0] = (grp(0) * conv).astype(BF16)
    cs_ref[0] = s_scr[pl.ds(tm + V7X_SUBLANES - (CONV_WIDTH - 1), CONV_WIDTH - 1), :]

    q_ref[0] = (grp(3) * sb_scale).astype(BF16)
    k = grp(4)
    k_ref[0] = k
    kb_ref[0] = k.astype(BF16)
    v = grp(5)
    v_ref[0] = v
    vb_ref[0] = v.astype(BF16)

    mq = grp(6)
    outs = []
    for hd in range(w // mem_hd):
        sl = slice(hd * mem_hd, (hd + 1) * mem_hd)
        s = _dot_nt(mq[:, sl].astype(BF16), mk_ref[0, :, sl]) * (mem_hd ** -0.5)
        p = _softmax_rows(s)
        outs.append(_dot(p.astype(BF16), mv_ref[0, :, sl]))
    yc_ref[0] = jnp.concatenate(outs, axis=1).astype(BF16)


def _inproj_prompt(x, g, w_a, conv_w, conv_b, mkb, mvb, *, sb_scale, mem_hd, tm):
    b, t, d = x.shape
    w = conv_w.shape[1]
    n_mem = mkb.shape[1]
    row = lambda dt: jax.ShapeDtypeStruct((b, t, w), dt)
    blk = pl.BlockSpec((1, tm, w), lambda i, j: (i, j, 0))
    body = functools.partial(_inproj_prompt_body, tm=tm, w=w, sb_scale=sb_scale, mem_hd=mem_hd)
    return pl.pallas_call(
        body,
        grid=(b, t // tm),
        in_specs=[pl.BlockSpec((1, tm, d), lambda i, j: (i, j, 0)), _const_spec((1, d)),
                  _const_spec(w_a.shape), _const_spec(conv_w.shape), _const_spec((1, w)),
                  pl.BlockSpec((1, n_mem, w), lambda i, j: (i, 0, 0)),
                  pl.BlockSpec((1, n_mem, w), lambda i, j: (i, 0, 0))],
        out_specs=[blk] * 7 + [pl.BlockSpec((1, CONV_WIDTH - 1, w), lambda i, j: (i, 0, 0))],
        out_shape=[row(BF16), row(BF16), row(F32), row(F32), row(BF16), row(BF16), row(BF16),
                   jax.ShapeDtypeStruct((b, CONV_WIDTH - 1, w), F32)],
        scratch_shapes=[pltpu.VMEM((tm + V7X_SUBLANES, w), F32)],
        compiler_params=_params("arbitrary", "arbitrary"),
        name="inproj_prompt",
    )(x, g, w_a, conv_w, conv_b, mkb, mvb)


def _inproj_sample_body(x_ref, g_ref, w_ref, cw_ref, cb_ref, st0_ref, st1_ref,
                        ya_ref, q_ref, k_ref, v_ref, mq_ref, ccu_ref, *, w, sb_scale):
    h = _rms(x_ref[...], g_ref[...]).astype(BF16)

    def grp(i):
        return _dot(h, w_ref[:, i * w:(i + 1) * w])

    ccu = grp(1) * grp(2)
    cw = cw_ref[...]
    conv = cb_ref[...] + cw[2:3] * ccu
    conv = conv + cw[0:1] * st0_ref[...]
    conv = conv + cw[1:2] * st1_ref[...]
    ya_ref[...] = (grp(0) * conv).astype(BF16)
    ccu_ref[...] = ccu
    q_ref[...] = grp(3) * sb_scale
    k_ref[...] = grp(4)
    v_ref[...] = grp(5)
    mq_ref[...] = grp(6)


def _inproj_sample(x, g, w_a, conv_w, conv_b, st0, st1, *, sb_scale):
    n, d = x.shape
    w = conv_w.shape[1]
    row = lambda dt: jax.ShapeDtypeStruct((n, w), dt)
    body = functools.partial(_inproj_sample_body, w=w, sb_scale=sb_scale)
    return pl.pallas_call(
        body,
        grid=(1,),
        in_specs=[_const_spec((n, d)), _const_spec((1, d)), _const_spec(w_a.shape),
                  _const_spec(conv_w.shape), _const_spec((1, w)), _const_spec((n, w)),
                  _const_spec((n, w))],
        out_specs=[_const_spec((n, w))] * 6,
        out_shape=[row(BF16), row(F32), row(F32), row(F32), row(F32), row(F32)],
        compiler_params=_params("arbitrary"),
        name="inproj_sample",
    )(x, g, w_a, conv_w, conv_b, st0, st1)


_SIGN_BIT = -2 ** 31


def _sb_log_weights(z, causal, uu):
    neg_abs = lax.bitcast_convert_type(lax.bitcast_convert_type(z, jnp.int32) | _SIGN_BIT, F32)
    sp = jnp.maximum(z, 0.0) + jnp.log(1.0 + jnp.exp(neg_abs))
    if causal is not None:
        sp = jnp.where(causal, sp, 0.0)
    hi = sp.astype(BF16)
    lo = (sp - hi.astype(F32)).astype(BF16)
    incl = _dot(jnp.concatenate([hi, lo], axis=1), uu)
    return z - incl, jnp.sum(sp, axis=-1, keepdims=True)


def _lower_tri(n, strict):
    row = lax.broadcasted_iota(jnp.int32, (n, n), 0)
    col = lax.broadcasted_iota(jnp.int32, (n, n), 1)
    return row > col if strict else row >= col


def _cumsum_weights(n):
    u = jnp.where(_lower_tri(n, strict=False), 1.0, 0.0).astype(BF16)
    return jnp.concatenate([u, u], axis=0)


def _sb_prompt_body(bias_ref, q_ref, k_ref, v_ref, o_ref, qs_scr, acc_scr, carry_scr, *, tq, hd):
    hp = pl.program_id(1)
    i = pl.program_id(2)
    hpb = V7X_LANES // hd
    q2 = q_ref[0]
    lane_head = lax.broadcasted_iota(jnp.int32, (tq, V7X_LANES), 1) // hd
    for h in range(hpb):
        qs_scr[h * tq:(h + 1) * tq, :] = jnp.where(lane_head == h, q2, jnp.zeros_like(q2))
    strict = _lower_tri(tq, strict=True)
    causal = jnp.concatenate([strict] * hpb, axis=0)
    uu = _cumsum_weights(tq)
    acc_scr[...] = jnp.zeros_like(acc_scr)
    carry_scr[...] = jnp.zeros_like(carry_scr)

    def log_weights(j, mask):
        ks = pl.multiple_of(j * tq, tq)
        mm = _dot_nt(qs_scr[...], k_ref[0, pl.ds(ks, tq), :])
        z = jnp.concatenate([mm[h * tq:(h + 1) * tq] + bias_ref[hp * hpb + h] for h in range(hpb)], axis=0)
        return _sb_log_weights(z, mask, uu)

    def values(j):
        return v_ref[0, pl.ds(pl.multiple_of(j * tq, tq), tq), :]

    def single(j, mask):
        lw, rest = log_weights(j, mask)
        carry = carry_scr[...]
        a = jnp.exp(lw - carry)
        if mask is not None:
            a = jnp.where(mask, a, 0.0)
        acc_scr[...] += _dot(a.astype(BF16), values(j))
        carry_scr[...] = carry + rest

    def pair(j):
        lw1, rest1 = log_weights(j, None)
        lw0, rest0 = log_weights(j - 1, None)
        carry = carry_scr[...]
        a1 = jnp.exp(lw1 - carry).astype(BF16)
        a0 = jnp.exp(lw0 - (carry + rest1)).astype(BF16)
        acc_scr[...] += _dot(jnp.concatenate([a1, a0], axis=1),
                             jnp.concatenate([values(j), values(j - 1)], axis=0))
        carry_scr[...] = carry + rest1 + rest0

    single(i, causal)
    odd = lax.bitwise_and(i, 1)

    @pl.when(odd == 1)
    def _():
        single(i - 1, None)

    def loop_body(n, c):
        pair(i - 1 - odd - 2 * n)
        return c

    lax.fori_loop(0, lax.shift_right_logical(i, 1), loop_body, 0)

    out = acc_scr[(hpb - 1) * tq:, :]
    for h in range(hpb - 2, -1, -1):
        out = jnp.where(lane_head == h, acc_scr[h * tq:(h + 1) * tq, :], out)
    o_ref[0] = out.astype(BF16)


def _sb_prompt(q, k, v, bias, *, hd, tq):
    b, t, w = q.shape
    body = functools.partial(_sb_prompt_body, tq=tq, hd=hd)
    hpb = V7X_LANES // hd
    grid_spec = pltpu.PrefetchScalarGridSpec(
        num_scalar_prefetch=1,
        grid=(b, w // V7X_LANES, t // tq),
        in_specs=[pl.BlockSpec((1, tq, V7X_LANES), lambda bi, hp, i, _: (bi, i, hp)),
                  pl.BlockSpec((1, t, V7X_LANES), lambda bi, hp, i, _: (bi, 0, hp)),
                  pl.BlockSpec((1, t, V7X_LANES), lambda bi, hp, i, _: (bi, 0, hp))],
        out_specs=pl.BlockSpec((1, tq, V7X_LANES), lambda bi, hp, i, _: (bi, i, hp)),
        scratch_shapes=[pltpu.VMEM((hpb * tq, V7X_LANES), BF16), pltpu.VMEM((hpb * tq, V7X_LANES), F32),
                        pltpu.VMEM((hpb * tq, 1), F32)],
    )
    return pl.pallas_call(
        body,
        grid_spec=grid_spec,
        out_shape=jax.ShapeDtypeStruct((b, t, w), BF16),
        compiler_params=_params("arbitrary", "arbitrary", "arbitrary"),
        name="sb_prompt",
    )(bias, q, k, v)


def _sb_sample_body(pt_ref, q_ref, bias_ref, *refs, pp, hd, rows):
    k_refs = refs[:pp]
    v_refs = refs[pp:2 * pp]
    o_ref = refs[2 * pp]
    acc_scr, carry_scr = refs[2 * pp + 1:]
    g = pl.program_id(1)
    w = q_ref.shape[-1]
    page = k_refs[0].shape[2]

    @pl.when(g == 0)
    def _():
        acc_scr[...] = jnp.zeros_like(acc_scr)
        carry_scr[...] = jnp.zeros_like(carry_scr)

    row_head = lax.broadcasted_iota(jnp.int32, (rows, w), 0)
    lane_head = lax.broadcasted_iota(jnp.int32, (rows, w), 1) // hd
    on_diag = row_head == lane_head
    qrow = jnp.broadcast_to(q_ref[0], (rows, w))
    qbd = jnp.where(on_diag, qrow, 0.0).astype(BF16)
    uu = _cumsum_weights(page)

    kt = jnp.concatenate([k_refs[r][0] for r in range(pp)], axis=1).astype(BF16)
    z = _dot(qbd, kt) + bias_ref[...]
    z_st = jnp.concatenate([z[:, r * page:(r + 1) * page] for r in range(pp)], axis=0)
    lw_st, rest_st = _sb_log_weights(z_st, None, uu)
    after = carry_scr[...]
    afters = [None] * pp
    for r in range(pp - 1, -1, -1):
        afters[r] = after
        after = after + rest_st[r * rows:(r + 1) * rows]
    carry_scr[...] = after
    a_st = jnp.exp(lw_st - jnp.concatenate(afters, axis=0)).astype(BF16)
    a = jnp.concatenate([a_st[r * rows:(r + 1) * rows] for r in range(pp)], axis=1)
    vt = jnp.concatenate([v_refs[r][0] for r in range(pp)], axis=1).astype(BF16)
    acc_scr[...] += _dot_nt(a, vt)

    @pl.when(g == pl.num_programs(1) - 1)
    def _():
        picked = jnp.where(on_diag, acc_scr[...], 0.0)
        o_ref[0] = jnp.sum(picked, axis=0, keepdims=True).astype(BF16)


def _sb_sample(q, bias_col, cache_kt, cache_vt, page_table, *, hd, pp, rows):
    n, w = q.shape
    n_pages = page_table.shape[1]
    page = cache_kt.shape[2]
    assert n_pages % pp == 0
    ng = n_pages // pp

    def page_spec(r):
        return pl.BlockSpec((1, w, page), lambda bi, g, pt: (pt[bi, (ng - 1 - g) * pp + r], 0, 0))

    body = functools.partial(_sb_sample_body, pp=pp, hd=hd, rows=rows)
    grid_spec = pltpu.PrefetchScalarGridSpec(
        num_scalar_prefetch=1,
        grid=(n, ng),
        in_specs=[pl.BlockSpec((1, 1, w), lambda bi, g, pt: (bi, 0, 0)),
                  pl.BlockSpec((rows, 1), lambda bi, g, pt: (0, 0))]
        + [page_spec(r) for r in range(pp)] * 2,
        out_specs=pl.BlockSpec((1, 1, w), lambda bi, g, pt: (bi, 0, 0)),
        scratch_shapes=[pltpu.VMEM((rows, w), F32), pltpu.VMEM((rows, 1), F32)],
    )
    out = pl.pallas_call(
        body,
        grid_spec=grid_spec,
        out_shape=jax.ShapeDtypeStruct((n, 1, w), BF16),
        compiler_params=_params("arbitrary", "arbitrary"),
        name="sb_sample",
    )(page_table, q.reshape(n, 1, w), bias_col, *([cache_kt] * pp), *([cache_vt] * pp))
    return out.reshape(n, w)


def _mem_sample_body(mq_ref, mk_ref, mv_ref, o_ref, *, mem_hd):
    q = mq_ref[0]
    prod = mk_ref[0] * q
    v = mv_ref[0]
    w = q.shape[-1]
    outs = []
    for h in range(w // mem_hd):
        sl = slice(h * mem_hd, (h + 1) * mem_hd)
        s = jnp.sum(prod[:, sl], axis=-1, keepdims=True) * (mem_hd ** -0.5)
        m = jnp.max(s, axis=0, keepdims=True)
        p = jnp.exp(s - m)
        p = p / jnp.sum(p, axis=0, keepdims=True)
        outs.append(jnp.sum(p * v[:, sl], axis=0, keepdims=True))
    o_ref[0] = jnp.concatenate(outs, axis=1).astype(BF16)


def _mem_sample(mq, mem_k, mem_v, *, mem_hd):
    n, w = mq.shape
    n_mem = mem_k.shape[1]
    row = pl.BlockSpec((1, 1, w), lambda i: (i, 0, 0))
    mem = pl.BlockSpec((1, n_mem, w), lambda i: (i, 0, 0))
    out = pl.pallas_call(
        functools.partial(_mem_sample_body, mem_hd=mem_hd),
        grid=(n,),
        in_specs=[row, mem, mem],
        out_specs=row,
        out_shape=jax.ShapeDtypeStruct((n, 1, w), BF16),
        compiler_params=_params("arbitrary"),
        name="mem_sample",
    )(mq.reshape(n, 1, w), mem_k, mem_v)
    return out.reshape(n, w)


def _mix_body(x_ref, ya_ref, yb_ref, yc_ref, gpre_ref, gpost_ref, wg_ref, wb_ref, wo_ref, o_ref):
    x = x_ref[0]
    h = _rms(x, gpre_ref[...]).astype(BF16)
    ys = (ya_ref[0], yb_ref[0], yc_ref[0])
    merged = None
    for n in range(N_BRANCH):
        gate = 1.0 / (1.0 + jnp.exp(-_dot(h, wg_ref[n])))
        term = gate * _dot(ys[n], wb_ref[n])
        merged = term if merged is None else merged + term
    mixed = _dot(merged.astype(BF16), wo_ref[...])
    o_ref[0] = x + _rms(mixed, gpost_ref[...])


def _mix(x, ya, yb, yc, g_pre, g_post, w_gate, w_branch, w_o, *, tm):
    b, t, d = x.shape
    w = ya.shape[-1]
    xblk = pl.BlockSpec((1, tm, d), lambda i, j: (i, j, 0))
    yblk = pl.BlockSpec((1, tm, w), lambda i, j: (i, j, 0))
    return pl.pallas_call(
        _mix_body,
        grid=(b, t // tm),
        in_specs=[xblk, yblk, yblk, yblk, _const_spec((1, d)), _const_spec((1, d)),
                  _const_spec(w_gate.shape), _const_spec(w_branch.shape), _const_spec(w_o.shape)],
        out_specs=xblk,
        out_shape=jax.ShapeDtypeStruct((b, t, d), F32),
        compiler_params=_params("arbitrary", "arbitrary"),
        name="mix",
    )(x, ya, yb, yc, g_pre, g_post, w_gate, w_branch, w_o)


def _gelu_tanh(x):
    return x * (0.5 * (1.0 + jnp.tanh((2.0 / jnp.pi) ** 0.5 * (x + 0.044715 * (x * x * x)))))


def _ffn_prompt_body(x_ref, gpre_ref, gpost_ref, wup_ref, fcw_ref, fcb_ref, wdn_ref,
                     o_ref, st_ref, hn_scr, act_scr, carry_scr, s_scr, *, tm, nc):
    ti = pl.program_id(1)
    width = wup_ref.shape[2]

    @pl.when(ti == 0)
    def _():
        carry_scr[...] = jnp.zeros_like(carry_scr)

    x = x_ref[0]
    hn_scr[...] = _rms(x, gpre_ref[...]).astype(BF16)

    def conv_chunk(idx, slot):
        u = _dot(hn_scr[...], wup_ref[idx])
        s_scr[slot, 0:V7X_SUBLANES, :] = carry_scr[idx]
        s_scr[slot, V7X_SUBLANES:, :] = u
        last = u[tm - V7X_SUBLANES:, :]
        carry_scr[idx] = last
        st_ref[0, idx] = last
        r1 = s_scr[slot, pl.ds(V7X_SUBLANES - 1, tm), :]
        r2 = s_scr[slot, pl.ds(V7X_SUBLANES - 2, tm), :]
        cw = fcw_ref[idx]
        y = fcb_ref[idx] + cw[2:3] * u
        y = y + cw[0:1] * r2
        return y + cw[1:2] * r1

    for c in range(nc):
        a = conv_chunk(c, 2 * (c % 2))
        bv = conv_chunk(c + nc, 2 * (c % 2) + 1)
        act_scr[:, c * width:(c + 1) * width] = (_gelu_tanh(a) * bv).astype(BF16)
    f = _dot(act_scr[...], wdn_ref[...])
    o_ref[0] = x + _rms(f, gpost_ref[...])


def _ffn_prompt(x, g_pre, g_post, w_up_c, fcw_c, fcb_c, w_dn, *, tm):
    b, t, d = x.shape
    nc2, _, cw = w_up_c.shape
    nc = nc2 // 2
    xblk = pl.BlockSpec((1, tm, d), lambda i, j: (i, j, 0))
    body = functools.partial(_ffn_prompt_body, tm=tm, nc=nc)
    return pl.pallas_call(
        body,
        grid=(b, t // tm),
        in_specs=[xblk, _const_spec((1, d)), _const_spec((1, d)), _const_spec(w_up_c.shape),
                  _const_spec(fcw_c.shape), _const_spec(fcb_c.shape), _const_spec(w_dn.shape)],
        out_specs=[xblk, pl.BlockSpec((1, nc2, V7X_SUBLANES, cw), lambda i, j: (i, 0, 0, 0))],
        out_shape=[jax.ShapeDtypeStruct((b, t, d), F32),
                   jax.ShapeDtypeStruct((b, nc2, V7X_SUBLANES, cw), F32)],
        scratch_shapes=[pltpu.VMEM((tm, d), BF16), pltpu.VMEM((tm, nc * cw), BF16),
                        pltpu.VMEM((nc2, V7X_SUBLANES, cw), F32),
                        pltpu.VMEM((4, tm + V7X_SUBLANES, cw), F32)],
        compiler_params=_params("arbitrary", "arbitrary"),
        name="ffn_prompt",
    )(x, g_pre, g_post, w_up_c, fcw_c, fcb_c, w_dn)


def _ffn_sample_body(x_ref, gpre_ref, gpost_ref, wup_ref, fcw_ref, fcb_ref, wdn_ref, st0_ref, st1_ref,
                     o_ref, up_ref, hn_scr, f_scr, *, nc):
    x = x_ref[...]
    hn_scr[...] = _rms(x, gpre_ref[...]).astype(BF16)
    f_scr[...] = jnp.zeros_like(f_scr)

    def conv_chunk(idx):
        u = _dot(hn_scr[...], wup_ref[idx])
        up_ref[idx] = u
        cw = fcw_ref[idx]
        y = fcb_ref[idx] + cw[2:3] * u
        y = y + cw[0:1] * st0_ref[idx]
        return y + cw[1:2] * st1_ref[idx]

    def chunk(c, carry):
        act = (_gelu_tanh(conv_chunk(c)) * conv_chunk(c + nc)).astype(BF16)
        f_scr[...] += _dot(act, wdn_ref[c])
        return carry

    lax.fori_loop(0, nc, chunk, 0)
    o_ref[...] = x + _rms(f_scr[...], gpost_ref[...])


def _ffn_sample(x, g_pre, g_post, w_up_c, fcw_c, fcb_c, w_dn_c, st0_c, st1_c):
    n, d = x.shape
    nc2, _, cw = w_up_c.shape
    body = functools.partial(_ffn_sample_body, nc=nc2 // 2)
    return pl.pallas_call(
        body,
        grid=(1,),
        in_specs=[_const_spec((n, d)), _const_spec((1, d)), _const_spec((1, d)), _const_spec(w_up_c.shape),
                  _const_spec(fcw_c.shape), _const_spec(fcb_c.shape), _const_spec(w_dn_c.shape),
                  _const_spec(st0_c.shape), _const_spec(st1_c.shape)],
        out_specs=[_const_spec((n, d)), _const_spec((nc2, n, cw))],
        out_shape=[jax.ShapeDtypeStruct((n, d), F32), jax.ShapeDtypeStruct((nc2, n, cw), F32)],
        scratch_shapes=[pltpu.VMEM((n, d), BF16), pltpu.VMEM((n, d), F32)],
        compiler_params=_params("arbitrary"),
        name="ffn_sample",
    )(x, g_pre, g_post, w_up_c, fcw_c, fcb_c, w_dn_c, st0_c, st1_c)


FFN_CHUNK = 2 * V7X_LANES
SAMPLE_PAGES_PER_STEP = 16
SAMPLE_HEAD_ROWS = 2 * V7X_SUBLANES


def _chunk_cols(a, cw):
    n = a.shape[-1]
    a = a.reshape(a.shape[:-1] + (n // cw, cw))
    return jnp.moveaxis(a, -2, 0)


def _unchunk_cols(a):
    nc, r, cw = a.shape
    return jnp.moveaxis(a, 0, 1).reshape(r, nc * cw)


def kernel(x_prompt, x_sample, cache_sb_k, cache_sb_v, cache_mem_k, cache_mem_v, state_conv, state_ffn_conv,
           page_table, mem_prompt, g_mix_pre, g_mix_post, g_ffn_pre, g_ffn_post, g_mem, w_in, conv_w, conv_b,
           sb_bias, w_mem_kv, w_branch, w_o, w_up, ffn_conv_w, ffn_conv_b, w_down):
    depth = w_in.shape[0]
    bp, seq, d = x_prompt.shape
    bs, dec_seq, _ = x_sample.shape
    assert dec_seq == 1, "the sample group advances one token per step"
    _, n_pool, page, sb_heads, sb_hd = cache_sb_k.shape
    _, _, n_mem, mem_heads, mem_hd = cache_mem_k.shape
    w = sb_heads * sb_hd
    d_ff = w_down.shape[1]
    assert mem_hd % V7X_LANES == 0 and V7X_LANES % sb_hd == 0 and d_ff % FFN_CHUNK == 0
    sb_scale = sb_hd ** -0.5
    assert math.frexp(sb_scale)[0] == 0.5, "the logit scale must be a power of two to fold into q exactly"

    tm_in = _pick_tile(seq, 512)
    tm_mix = _pick_tile(seq, 512)
    tm_ffn = _pick_tile(seq, 256)
    tq = _pick_tile(seq, 256)

    xp = x_prompt
    xs = x_sample.reshape(bs, d)
    conv_p, k_p, v_p, mk_p, mv_p, ffn_p = [], [], [], [], [], []
    conv_s, k_s, v_s, ffn_s = [], [], [], []
    for l in range(depth):
        row = lambda a: a[l].reshape(1, -1)
        w_a = w_in[l][:, :7 * w].astype(BF16)
        w_gate = jnp.moveaxis(w_in[l][:, 7 * w:].reshape(d, N_BRANCH, d), 1, 0).astype(BF16)
        w_br = w_branch[l].astype(BF16)
        w_out = w_o[l].astype(BF16)
        w_up_c = _chunk_cols(w_up[l], FFN_CHUNK).astype(BF16)
        w_dn_c = w_down[l].reshape(d_ff // FFN_CHUNK, FFN_CHUNK, d).astype(BF16)
        fcw_c = _chunk_cols(ffn_conv_w[l], FFN_CHUNK)
        fcb_c = _chunk_cols(ffn_conv_b[l].reshape(1, -1), FFN_CHUNK)
        bias = sb_bias[l]

        mk, mv, mkb, mvb = _memkv(mem_prompt, row(g_mem), w_mem_kv[l].astype(BF16))
        ya, q, k, v, kb, vb, yc, cs = _inproj_prompt(
            xp, row(g_mix_pre), w_a, conv_w[l], row(conv_b), mkb, mvb,
            sb_scale=sb_scale, mem_hd=mem_hd, tm=tm_in)
        yb = _sb_prompt(q, kb, vb, bias, hd=sb_hd, tq=tq)
        x1 = _mix(xp, ya, yb, yc, row(g_mix_pre), row(g_mix_post), w_gate, w_br, w_out, tm=tm_mix)
        xp, st = _ffn_prompt(x1, row(g_ffn_pre), row(g_ffn_post), w_up_c, fcw_c, fcb_c,
                             w_dn_c.reshape(d_ff, d), tm=tm_ffn)
        fs = jnp.moveaxis(st[:, :, V7X_SUBLANES - (CONV_WIDTH - 1):, :], 1, 2).reshape(bp, CONV_WIDTH - 1, -1)
        conv_p.append(cs)
        k_p.append(k.reshape(bp, seq, sb_heads, sb_hd))
        v_p.append(v.reshape(bp, seq, sb_heads, sb_hd))
        mk_p.append(mk.reshape(bp, n_mem, mem_heads, mem_hd))
        mv_p.append(mv.reshape(bp, n_mem, mem_heads, mem_hd))
        ffn_p.append(fs)

        st_c = state_conv[l]
        ya, q, k, v, mq, ccu = _inproj_sample(xs, row(g_mix_pre), w_a, conv_w[l], row(conv_b),
                                              st_c[:, 0], st_c[:, 1], sb_scale=sb_scale)
        bias_col = jnp.zeros((SAMPLE_HEAD_ROWS, 1), F32).at[:sb_heads, 0].set(bias)
        feature_major = lambda c: jnp.transpose(c[l], (0, 2, 3, 1)).reshape(n_pool, w, page)
        yb = _sb_sample(q, bias_col, feature_major(cache_sb_k), feature_major(cache_sb_v),
                        page_table, hd=sb_hd, pp=SAMPLE_PAGES_PER_STEP, rows=SAMPLE_HEAD_ROWS)
        yc = _mem_sample(mq, cache_mem_k[l].reshape(bs, n_mem, w), cache_mem_v[l].reshape(bs, n_mem, w),
                         mem_hd=mem_hd)
        x1 = _mix(xs[None], ya[None], yb[None], yc[None], row(g_mix_pre), row(g_mix_post),
                  w_gate, w_br, w_out, tm=bs)[0]
        st_f = state_ffn_conv[l]
        xs, up_c = _ffn_sample(x1, row(g_ffn_pre), row(g_ffn_post), w_up_c, fcw_c, fcb_c, w_dn_c,
                               _chunk_cols(st_f[:, 0], FFN_CHUNK), _chunk_cols(st_f[:, 1], FFN_CHUNK))
        conv_s.append(jnp.stack([st_c[:, 1], ccu], axis=1))
        k_s.append(k.reshape(bs, 1, sb_heads, sb_hd))
        v_s.append(v.reshape(bs, 1, sb_heads, sb_hd))
        ffn_s.append(jnp.stack([st_f[:, 1], _unchunk_cols(up_c)], axis=1))

    return (xp, xs.reshape(bs, 1, d), jnp.stack(conv_p), jnp.stack(k_p), jnp.stack(v_p), jnp.stack(mk_p),
            jnp.stack(mv_p), jnp.stack(ffn_p), jnp.stack(conv_s), jnp.stack(k_s), jnp.stack(v_s),
            jnp.stack(ffn_s))
```

```python
import functools
import math

import jax
import jax.numpy as jnp
from jax import lax
from jax.experimental import pallas as pl
from jax.experimental.pallas import tpu as pltpu

EPS = 1e-6
CONV_WIDTH = 3
N_BRANCH = 3
F32 = jnp.float32
BF16 = jnp.bfloat16

V7X_LANES = 128
V7X_SUBLANES = 8
V7X_VMEM_BYTES = 64 * 1024 * 1024
VMEM_LIMIT_BYTES = (V7X_VMEM_BYTES * 7) // 8


def _params(*semantics):
    return pltpu.CompilerParams(dimension_semantics=semantics, vmem_limit_bytes=VMEM_LIMIT_BYTES)


def _rms(x, g):
    ms = jnp.mean(x * x, axis=-1, keepdims=True)
    return (x * lax.rsqrt(ms + EPS)) * g


def _dot(a, b):
    return jnp.dot(a, b, preferred_element_type=F32)


def _dot_nt(a, b):
    return lax.dot_general(a, b, (((1,), (1,)), ((), ())), preferred_element_type=F32)


def _const_spec(shape):
    nd = len(shape)
    return pl.BlockSpec(shape, lambda *_: (0,) * nd, pipeline_mode=pl.Buffered(1))


def _pick_tile(n, want):
    t = min(n, want)
    assert n % t == 0, (n, t)
    return t


def _memkv_body(mem_ref, g_ref, w_ref, mk_ref, mv_ref, mkb_ref, mvb_ref):
    h = _rms(mem_ref[0], g_ref[...]).astype(BF16)
    kv = _dot(h, w_ref[...])
    w = kv.shape[1] // 2
    mk_ref[0] = kv[:, :w]
    mv_ref[0] = kv[:, w:]
    mkb_ref[0] = kv[:, :w].astype(BF16)
    mvb_ref[0] = kv[:, w:].astype(BF16)


def _memkv(mem, g_mem, w_mem_kv_bf):
    b, n_mem, d = mem.shape
    w = w_mem_kv_bf.shape[1] // 2
    blk = pl.BlockSpec((1, n_mem, w), lambda i: (i, 0, 0))
    return pl.pallas_call(
        _memkv_body,
        grid=(b,),
        in_specs=[pl.BlockSpec((1, n_mem, d), lambda i: (i, 0, 0)), _const_spec((1, d)),
                  _const_spec((d, 2 * w))],
        out_specs=[blk, blk, blk, blk],
        out_shape=[jax.ShapeDtypeStruct((b, n_mem, w), F32)] * 2
        + [jax.ShapeDtypeStruct((b, n_mem, w), BF16)] * 2,
        compiler_params=_params("arbitrary"),
        name="memkv",
    )(mem, g_mem, w_mem_kv_bf)


def _softmax_rows(s):
    m = jnp.max(s, axis=-1, keepdims=True)
    p = jnp.exp(s - m)
    return p / jnp.sum(p, axis=-1, keepdims=True)


def _inproj_prompt_body(x_ref, g_ref, w_ref, cw_ref, cb_ref, mk_ref, mv_ref,
                        ya_ref, q_ref, k_ref, v_ref, kb_ref, vb_ref, yc_ref, cs_ref,
                        s_scr, *, tm, w, sb_scale, mem_hd):
    ti = pl.program_id(1)
    h = _rms(x_ref[0], g_ref[...]).astype(BF16)

    def grp(i):
        return _dot(h, w_ref[:, i * w:(i + 1) * w])

    ccu = grp(1) * grp(2)
    @pl.when(ti == 0)
    def _():
        s_scr[tm:tm + V7X_SUBLANES, :] = jnp.zeros((V7X_SUBLANES, w), F32)

    s_scr[0:V7X_SUBLANES, :] = s_scr[tm:tm + V7X_SUBLANES, :]
    s_scr[V7X_SUBLANES:, :] = ccu
    r1 = s_scr[pl.ds(V7X_SUBLANES - 1, tm), :]
    r2 = s_scr[pl.ds(V7X_SUBLANES - 2, tm), :]
    cw = cw_ref[...]
    conv = cb_ref[...] + cw[2:3] * ccu
    conv = conv + cw[0:1] * r2
    conv = conv + cw[1:2] * r1
    ya_ref[0] = (grp(0) * conv).astype(BF16)
    cs_ref[0] = s_scr[pl.ds(tm + V7X_SUBLANES - (CONV_WIDTH - 1), CONV_WIDTH - 1), :]

    q_ref[0] = (grp(3) * sb_scale).astype(BF16)
    k = grp(4)
    k_ref[0] = k
    kb_ref[0] = k.astype(BF16)
    v = grp(5)
    v_ref[0] = v
    vb_ref[0] = v.astype(BF16)

    mq = grp(6)
    outs = []
    for hd in range(w // mem_hd):
        sl = slice(hd * mem_hd, (hd + 1) * mem_hd)
        s = _dot_nt(mq[:, sl].astype(BF16), mk_ref[0, :, sl]) * (mem_hd ** -0.5)
        p = _softmax_rows(s)
        outs.append(_dot(p.astype(BF16), mv_ref[0, :, sl]))
    yc_ref[0] = jnp.concatenate(outs, axis=1).astype(BF16)


def _inproj_prompt(x, g, w_a, conv_w, conv_b, mkb, mvb, *, sb_scale, mem_hd, tm):
    b, t, d = x.shape
    w = conv_w.shape[1]
    n_mem = mkb.shape[1]
    row = lambda dt: jax.ShapeDtypeStruct((b, t, w), dt)
    blk = pl.BlockSpec((1, tm, w), lambda i, j: (i, j, 0))
    body = functools.partial(_inproj_prompt_body, tm=tm, w=w, sb_scale=sb_scale, mem_hd=mem_hd)
    return pl.pallas_call(
        body,
        grid=(b, t // tm),
        in_specs=[pl.BlockSpec((1, tm, d), lambda i, j: (i, j, 0)), _const_spec((1, d)),
                  _const_spec(w_a.shape), _const_spec(conv_w.shape), _const_spec((1, w)),
                  pl.BlockSpec((1, n_mem, w), lambda i, j: (i, 0, 0)),
                  pl.BlockSpec((1, n_mem, w), lambda i, j: (i, 0, 0))],
        out_specs=[blk] * 7 + [pl.BlockSpec((1, CONV_WIDTH - 1, w), lambda i, j: (i, 0, 0))],
        out_shape=[row(BF16), row(BF16), row(F32), row(F32), row(BF16), row(BF16), row(BF16),
                   jax.ShapeDtypeStruct((b, CONV_WIDTH - 1, w), F32)],
        scratch_shapes=[pltpu.VMEM((tm + V7X_SUBLANES, w), F32)],
        compiler_params=_params("arbitrary", "arbitrary"),
        name="inproj_prompt",
    )(x, g, w_a, conv_w, conv_b, mkb, mvb)


def _inproj_sample_body(x_ref, g_ref, w_ref, cw_ref, cb_ref, st0_ref, st1_ref,
                        ya_ref, q_ref, k_ref, v_ref, mq_ref, ccu_ref, *, w, sb_scale):
    h = _rms(x_ref[...], g_ref[...]).astype(BF16)

    def grp(i):
        return _dot(h, w_ref[:, i * w:(i + 1) * w])

    ccu = grp(1) * grp(2)
    cw = cw_ref[...]
    conv = cb_ref[...] + cw[2:3] * ccu
    conv = conv + cw[0:1] * st0_ref[...]
    conv = conv + cw[1:2] * st1_ref[...]
    ya_ref[...] = (grp(0) * conv).astype(BF16)
    ccu_ref[...] = ccu
    q_ref[...] = grp(3) * sb_scale
    k_ref[...] = grp(4)
    v_ref[...] = grp(5)
    mq_ref[...] = grp(6)


def _inproj_sample(x, g, w_a, conv_w, conv_b, st0, st1, *, sb_scale):
    n, d = x.shape
    w = conv_w.shape[1]
    row = lambda dt: jax.ShapeDtypeStruct((n, w), dt)
    body = functools.partial(_inproj_sample_body, w=w, sb_scale=sb_scale)
    return pl.pallas_call(
        body,
        grid=(1,),
        in_specs=[_const_spec((n, d)), _const_spec((1, d)), _const_spec(w_a.shape),
                  _const_spec(conv_w.shape), _const_spec((1, w)), _const_spec((n, w)),
                  _const_spec((n, w))],
        out_specs=[_const_spec((n, w))] * 6,
        out_shape=[row(BF16), row(F32), row(F32), row(F32), row(F32), row(F32)],
        compiler_params=_params("arbitrary"),
        name="inproj_sample",
    )(x, g, w_a, conv_w, conv_b, st0, st1)


_SIGN_BIT = -2 ** 31
_NEG_HUGE = -1e30


def _softplus(z):
    neg_abs = lax.bitcast_convert_type(lax.bitcast_convert_type(z, jnp.int32) | _SIGN_BIT, F32)
    return jnp.maximum(z, 0.0) + jnp.log(1.0 + jnp.exp(neg_abs))


def _sb_log_weights(z, causal, u):
    sp = _softplus(z)
    if causal is not None:
        sp = jnp.where(causal, sp, 0.0)
    incl = _dot(sp.astype(BF16), u)
    return z - incl, jnp.sum(sp, axis=-1, keepdims=True)


def _lower_tri(n, strict):
    row = lax.broadcasted_iota(jnp.int32, (n, n), 0)
    col = lax.broadcasted_iota(jnp.int32, (n, n), 1)
    return row > col if strict else row >= col


def _cumsum_weights(n):
    return jnp.where(_lower_tri(n, strict=False), 1.0, 0.0).astype(BF16)


def _sb_prompt_body(bias_ref, q_ref, k_ref, v_ref, o_ref, qs_scr, acc_scr, carry_scr,
                    lwa_scr, lwb_scr, resta_scr, restb_scr, *, tq, tk, hd):
    hp = pl.program_id(1)
    i = pl.program_id(2)
    hpb = V7X_LANES // hd
    ratio = tq // tk
    q2 = q_ref[0]
    lane_head = lax.broadcasted_iota(jnp.int32, (tq, V7X_LANES), 1) // hd
    for h in range(hpb):
        qs_scr[h * tq:(h + 1) * tq, :] = jnp.where(lane_head == h, q2, jnp.zeros_like(q2))
    u = _cumsum_weights(tk)
    acc_scr[...] = jnp.zeros_like(acc_scr)
    carry_scr[...] = jnp.zeros_like(carry_scr)

    def log_weights(j, mask):
        ks = pl.multiple_of(j * tk, tk)
        mm = _dot_nt(qs_scr[...], k_ref[0, pl.ds(ks, tk), :])
        z = jnp.concatenate([mm[h * tq:(h + 1) * tq] + bias_ref[hp * hpb + h] for h in range(hpb)], axis=0)
        return _sb_log_weights(z, mask, u)

    def values(j):
        return v_ref[0, pl.ds(pl.multiple_of(j * tk, tk), tk), :]

    def chain(j_top, masks):
        n = len(masks)
        parts = [log_weights(j_top - t, masks[t]) for t in range(n)]
        carry = carry_scr[...]
        weights = []
        for t, (lw, rest) in enumerate(parts):
            a = jnp.exp(lw - carry)
            if masks[t] is not None:
                a = jnp.where(masks[t], a, 0.0)
            weights.append(a.astype(BF16))
            carry = carry + rest
        acc_scr[...] += _dot(jnp.concatenate(weights, axis=1),
                             jnp.concatenate([values(j_top - t) for t in range(n)], axis=0))
        carry_scr[...] = carry

    row = lax.broadcasted_iota(jnp.int32, (tq, tk), 0)
    col = lax.broadcasted_iota(jnp.int32, (tq, tk), 1)
    diag_masks = [jnp.concatenate([col + d * tk < row] * hpb, axis=0) for d in range(ratio - 1, -1, -1)]
    n_full = ratio * i
    odd = lax.bitwise_and(n_full, 1)
    n_pairs = lax.shift_right_logical(n_full, 1)
    top = n_full - 1 - odd

    def front(p, lw_ref, rest_ref):
        for t in range(2):
            j = jnp.maximum(top - 2 * p - t, 0)
            lw, rest = log_weights(j, None)
            lw_ref[t] = lw
            rest_ref[t] = rest

    def back(p, lw_ref, rest_ref):
        j = top - 2 * p
        carry = carry_scr[...]
        rest1 = rest_ref[0]
        a1 = jnp.exp(lw_ref[0] - carry).astype(BF16)
        a0 = jnp.exp(lw_ref[1] - (carry + rest1)).astype(BF16)
        acc_scr[...] += _dot(jnp.concatenate([a1, a0], axis=1),
                             jnp.concatenate([values(j), values(j - 1)], axis=0))
        carry_scr[...] = carry + rest1 + rest_ref[1]

    front(0, lwa_scr, resta_scr)
    chain(ratio * i + ratio - 1, diag_masks)

    @pl.when(odd == 1)
    def _():
        chain(n_full - 1, [None])

    def loop_body(n, c):
        p = 2 * n
        front(p + 1, lwb_scr, restb_scr)
        back(p, lwa_scr, resta_scr)

        @pl.when(p + 1 < n_pairs)
        def _():
            front(p + 2, lwa_scr, resta_scr)
            back(p + 1, lwb_scr, restb_scr)

        return c

    lax.fori_loop(0, lax.shift_right_logical(n_pairs + 1, 1), loop_body, 0)

    out = acc_scr[(hpb - 1) * tq:, :]
    for h in range(hpb - 2, -1, -1):
        out = jnp.where(lane_head == h, acc_scr[h * tq:(h + 1) * tq, :], out)
    o_ref[0] = out.astype(BF16)


def _sb_prompt(q, k, v, bias, *, hd, tq, tk):
    b, t, w = q.shape
    assert tq % tk == 0 and t % tq == 0
    body = functools.partial(_sb_prompt_body, tq=tq, tk=tk, hd=hd)
    hpb = V7X_LANES // hd
    pair = (2, hpb * tq)
    grid_spec = pltpu.PrefetchScalarGridSpec(
        num_scalar_prefetch=1,
        grid=(b, w // V7X_LANES, t // tq),
        in_specs=[pl.BlockSpec((1, tq, V7X_LANES), lambda bi, hp, i, _: (bi, i, hp)),
                  pl.BlockSpec((1, t, V7X_LANES), lambda bi, hp, i, _: (bi, 0, hp)),
                  pl.BlockSpec((1, t, V7X_LANES), lambda bi, hp, i, _: (bi, 0, hp))],
        out_specs=pl.BlockSpec((1, tq, V7X_LANES), lambda bi, hp, i, _: (bi, i, hp)),
        scratch_shapes=[pltpu.VMEM((hpb * tq, V7X_LANES), BF16), pltpu.VMEM((hpb * tq, V7X_LANES), F32),
                        pltpu.VMEM((hpb * tq, 1), F32),
                        pltpu.VMEM(pair + (tk,), F32), pltpu.VMEM(pair + (tk,), F32),
                        pltpu.VMEM(pair + (1,), F32), pltpu.VMEM(pair + (1,), F32)],
    )
    return pl.pallas_call(
        body,
        grid_spec=grid_spec,
        out_shape=jax.ShapeDtypeStruct((b, t, w), BF16),
        compiler_params=_params("arbitrary", "arbitrary", "arbitrary"),
        name="sb_prompt",
    )(bias, q, k, v)


def _sb_sample_body(pt_ref, q_ref, bias_ref, *refs, pp, hd, rows):
    k_refs = refs[:pp]
    v_refs = refs[pp:2 * pp]
    o_ref = refs[2 * pp]
    acc_scr, carry_scr = refs[2 * pp + 1:]
    g = pl.program_id(1)
    w = q_ref.shape[-1]
    page = k_refs[0].shape[2]

    @pl.when(g == 0)
    def _():
        acc_scr[...] = jnp.zeros_like(acc_scr)
        carry_scr[...] = jnp.zeros_like(carry_scr)

    row_head = lax.broadcasted_iota(jnp.int32, (rows, w), 0)
    lane_head = lax.broadcasted_iota(jnp.int32, (rows, w), 1) // hd
    on_diag = row_head == lane_head
    qrow = jnp.broadcast_to(q_ref[0], (rows, w))
    qbd = jnp.where(on_diag, qrow, 0.0).astype(BF16)
    u = _cumsum_weights(page)

    kt = jnp.concatenate([k_refs[r][0] for r in range(pp)], axis=1).astype(BF16)
    z = _dot(qbd, kt) + bias_ref[...]
    z_st = jnp.concatenate([z[:, r * page:(r + 1) * page] for r in range(pp)], axis=0)
    lw_st, rest_st = _sb_log_weights(z_st, None, u)
    after = carry_scr[...]
    afters = [None] * pp
    for r in range(pp - 1, -1, -1):
        afters[r] = after
        after = after + rest_st[r * rows:(r + 1) * rows]
    carry_scr[...] = after
    a_st = jnp.exp(lw_st - jnp.concatenate(afters, axis=0)).astype(BF16)
    a = jnp.concatenate([a_st[r * rows:(r + 1) * rows] for r in range(pp)], axis=1)
    vt = jnp.concatenate([v_refs[r][0] for r in range(pp)], axis=1).astype(BF16)
    acc_scr[...] += _dot_nt(a, vt)

    @pl.when(g == pl.num_programs(1) - 1)
    def _():
        picked = jnp.where(on_diag, acc_scr[...], 0.0)
        o_ref[0] = jnp.sum(picked, axis=0, keepdims=True).astype(BF16)


def _sb_sample(q, bias_col, cache_kt, cache_vt, page_table, *, hd, pp, rows):
    n, w = q.shape
    n_pages = page_table.shape[1]
    page = cache_kt.shape[2]
    assert n_pages % pp == 0
    ng = n_pages // pp

    def page_spec(r):
        return pl.BlockSpec((1, w, page), lambda bi, g, pt: (pt[bi, (ng - 1 - g) * pp + r], 0, 0))

    body = functools.partial(_sb_sample_body, pp=pp, hd=hd, rows=rows)
    grid_spec = pltpu.PrefetchScalarGridSpec(
        num_scalar_prefetch=1,
        grid=(n, ng),
        in_specs=[pl.BlockSpec((1, 1, w), lambda bi, g, pt: (bi, 0, 0)),
                  pl.BlockSpec((rows, 1), lambda bi, g, pt: (0, 0))]
        + [page_spec(r) for r in range(pp)] * 2,
        out_specs=pl.BlockSpec((1, 1, w), lambda bi, g, pt: (bi, 0, 0)),
        scratch_shapes=[pltpu.VMEM((rows, w), F32), pltpu.VMEM((rows, 1), F32)],
    )
    out = pl.pallas_call(
        body,
        grid_spec=grid_spec,
        out_shape=jax.ShapeDtypeStruct((n, 1, w), BF16),
        compiler_params=_params("arbitrary", "arbitrary"),
        name="sb_sample",
    )(page_table, q.reshape(n, 1, w), bias_col, *([cache_kt] * pp), *([cache_vt] * pp))
    return out.reshape(n, w)


def _mem_sample_body(mq_ref, mk_ref, mv_ref, o_ref, *, mem_hd):
    q = mq_ref[0]
    prod = mk_ref[0] * q
    v = mv_ref[0]
    w = q.shape[-1]
    outs = []
    for h in range(w // mem_hd):
        sl = slice(h * mem_hd, (h + 1) * mem_hd)
        s = jnp.sum(prod[:, sl], axis=-1, keepdims=True) * (mem_hd ** -0.5)
        m = jnp.max(s, axis=0, keepdims=True)
        p = jnp.exp(s - m)
        p = p / jnp.sum(p, axis=0, keepdims=True)
        outs.append(jnp.sum(p * v[:, sl], axis=0, keepdims=True))
    o_ref[0] = jnp.concatenate(outs, axis=1).astype(BF16)


def _mem_sample(mq, mem_k, mem_v, *, mem_hd):
    n, w = mq.shape
    n_mem = mem_k.shape[1]
    row = pl.BlockSpec((1, 1, w), lambda i: (i, 0, 0))
    mem = pl.BlockSpec((1, n_mem, w), lambda i: (i, 0, 0))
    out = pl.pallas_call(
        functools.partial(_mem_sample_body, mem_hd=mem_hd),
        grid=(n,),
        in_specs=[row, mem, mem],
        out_specs=row,
        out_shape=jax.ShapeDtypeStruct((n, 1, w), BF16),
        compiler_params=_params("arbitrary"),
        name="mem_sample",
    )(mq.reshape(n, 1, w), mem_k, mem_v)
    return out.reshape(n, w)


def _mix_body(x_ref, ya_ref, yb_ref, yc_ref, gpre_ref, gpost_ref, wg_ref, wb_ref, wo_ref, o_ref):
    x = x_ref[0]
    h = _rms(x, gpre_ref[...]).astype(BF16)
    ys = (ya_ref[0], yb_ref[0], yc_ref[0])
    merged = None
    for n in range(N_BRANCH):
        gate = 1.0 / (1.0 + jnp.exp(-_dot(h, wg_ref[n])))
        term = gate * _dot(ys[n], wb_ref[n])
        merged = term if merged is None else merged + term
    mixed = _dot(merged.astype(BF16), wo_ref[...])
    o_ref[0] = x + _rms(mixed, gpost_ref[...])


def _mix(x, ya, yb, yc, g_pre, g_post, w_gate, w_branch, w_o, *, tm):
    b, t, d = x.shape
    w = ya.shape[-1]
    xblk = pl.BlockSpec((1, tm, d), lambda i, j: (i, j, 0))
    yblk = pl.BlockSpec((1, tm, w), lambda i, j: (i, j, 0))
    return pl.pallas_call(
        _mix_body,
        grid=(b, t // tm),
        in_specs=[xblk, yblk, yblk, yblk, _const_spec((1, d)), _const_spec((1, d)),
                  _const_spec(w_gate.shape), _const_spec(w_branch.shape), _const_spec(w_o.shape)],
        out_specs=xblk,
        out_shape=jax.ShapeDtypeStruct((b, t, d), F32),
        compiler_params=_params("arbitrary", "arbitrary"),
        name="mix",
    )(x, ya, yb, yc, g_pre, g_post, w_gate, w_branch, w_o)


def _gelu_tanh(x):
    return x * (0.5 * (1.0 + jnp.tanh((2.0 / jnp.pi) ** 0.5 * (x + 0.044715 * (x * x * x)))))


def _ffn_prompt_body(x_ref, gpre_ref, gpost_ref, wup_ref, fcw_ref, fcb_ref, wdn_ref,
                     o_ref, st_ref, hn_scr, act_scr, carry_scr, s_scr, *, tm, nc):
    ti = pl.program_id(1)
    width = wup_ref.shape[2]

    @pl.when(ti == 0)
    def _():
        carry_scr[...] = jnp.zeros_like(carry_scr)

    x = x_ref[0]
    hn_scr[...] = _rms(x, gpre_ref[...]).astype(BF16)

    def conv_chunk(idx, slot):
        u = _dot(hn_scr[...], wup_ref[idx])
        s_scr[slot, 0:V7X_SUBLANES, :] = carry_scr[idx]
        s_scr[slot, V7X_SUBLANES:, :] = u
        last = u[tm - V7X_SUBLANES:, :]
        carry_scr[idx] = last
        st_ref[0, idx] = last
        r1 = s_scr[slot, pl.ds(V7X_SUBLANES - 1, tm), :]
        r2 = s_scr[slot, pl.ds(V7X_SUBLANES - 2, tm), :]
        cw = fcw_ref[idx]
        y = fcb_ref[idx] + cw[2:3] * u
        y = y + cw[0:1] * r2
        return y + cw[1:2] * r1

    for c in range(nc):
        a = conv_chunk(c, 2 * (c % 2))
        bv = conv_chunk(c + nc, 2 * (c % 2) + 1)
        act_scr[:, c * width:(c + 1) * width] = (_gelu_tanh(a) * bv).astype(BF16)
    f = _dot(act_scr[...], wdn_ref[...])
    o_ref[0] = x + _rms(f, gpost_ref[...])


def _ffn_prompt(x, g_pre, g_post, w_up_c, fcw_c, fcb_c, w_dn, *, tm):
    b, t, d = x.shape
    nc2, _, cw = w_up_c.shape
    nc = nc2 // 2
    xblk = pl.BlockSpec((1, tm, d), lambda i, j: (i, j, 0))
    body = functools.partial(_ffn_prompt_body, tm=tm, nc=nc)
    return pl.pallas_call(
        body,
        grid=(b, t // tm),
        in_specs=[xblk, _const_spec((1, d)), _const_spec((1, d)), _const_spec(w_up_c.shape),
                  _const_spec(fcw_c.shape), _const_spec(fcb_c.shape), _const_spec(w_dn.shape)],
        out_specs=[xblk, pl.BlockSpec((1, nc2, V7X_SUBLANES, cw), lambda i, j: (i, 0, 0, 0))],
        out_shape=[jax.ShapeDtypeStruct((b, t, d), F32),
                   jax.ShapeDtypeStruct((b, nc2, V7X_SUBLANES, cw), F32)],
        scratch_shapes=[pltpu.VMEM((tm, d), BF16), pltpu.VMEM((tm, nc * cw), BF16),
                        pltpu.VMEM((nc2, V7X_SUBLANES, cw), F32),
                        pltpu.VMEM((4, tm + V7X_SUBLANES, cw), F32)],
        compiler_params=_params("arbitrary", "arbitrary"),
        name="ffn_prompt",
    )(x, g_pre, g_post, w_up_c, fcw_c, fcb_c, w_dn)


def _ffn_sample_body(x_ref, gpre_ref, gpost_ref, wup_ref, fcw_ref, fcb_ref, wdn_ref, st0_ref, st1_ref,
                     o_ref, up_ref, hn_scr, f_scr, *, nc):
    x = x_ref[...]
    hn_scr[...] = _rms(x, gpre_ref[...]).astype(BF16)
    f_scr[...] = jnp.zeros_like(f_scr)

    def conv_chunk(idx):
        u = _dot(hn_scr[...], wup_ref[idx])
        up_ref[idx] = u
        cw = fcw_ref[idx]
        y = fcb_ref[idx] + cw[2:3] * u
        y = y + cw[0:1] * st0_ref[idx]
        return y + cw[1:2] * st1_ref[idx]

    def chunk(c, carry):
        act = (_gelu_tanh(conv_chunk(c)) * conv_chunk(c + nc)).astype(BF16)
        f_scr[...] += _dot(act, wdn_ref[c])
        return carry

    lax.fori_loop(0, nc, chunk, 0)
    o_ref[...] = x + _rms(f_scr[...], gpost_ref[...])


def _ffn_sample(x, g_pre, g_post, w_up_c, fcw_c, fcb_c, w_dn_c, st0_c, st1_c):
    n, d = x.shape
    nc2, _, cw = w_up_c.shape
    body = functools.partial(_ffn_sample_body, nc=nc2 // 2)
    return pl.pallas_call(
        body,
        grid=(1,),
        in_specs=[_const_spec((n, d)), _const_spec((1, d)), _const_spec((1, d)), _const_spec(w_up_c.shape),
                  _const_spec(fcw_c.shape), _const_spec(fcb_c.shape), _const_spec(w_dn_c.shape),
                  _const_spec(st0_c.shape), _const_spec(st1_c.shape)],
        out_specs=[_const_spec((n, d)), _const_spec((nc2, n, cw))],
        out_shape=[jax.ShapeDtypeStruct((n, d), F32), jax.ShapeDtypeStruct((nc2, n, cw), F32)],
        scratch_shapes=[pltpu.VMEM((n, d), BF16), pltpu.VMEM((n, d), F32)],
        compiler_params=_params("arbitrary"),
        name="ffn_sample",
    )(x, g_pre, g_post, w_up_c, fcw_c, fcb_c, w_dn_c, st0_c, st1_c)


FFN_CHUNK = 2 * V7X_LANES
SAMPLE_PAGES_PER_STEP = 16
SAMPLE_HEAD_ROWS = 2 * V7X_SUBLANES


def _chunk_cols(a, cw):
    n = a.shape[-1]
    a = a.reshape(a.shape[:-1] + (n // cw, cw))
    return jnp.moveaxis(a, -2, 0)


def _unchunk_cols(a):
    nc, r, cw = a.shape
    return jnp.moveaxis(a, 0, 1).reshape(r, nc * cw)


def kernel(x_prompt, x_sample, cache_sb_k, cache_sb_v, cache_mem_k, cache_mem_v, state_conv, state_ffn_conv,
           page_table, mem_prompt, g_mix_pre, g_mix_post, g_ffn_pre, g_ffn_post, g_mem, w_in, conv_w, conv_b,
           sb_bias, w_mem_kv, w_branch, w_o, w_up, ffn_conv_w, ffn_conv_b, w_down):
    depth = w_in.shape[0]
    bp, seq, d = x_prompt.shape
    bs, dec_seq, _ = x_sample.shape
    assert dec_seq == 1, "the sample group advances one token per step"
    _, n_pool, page, sb_heads, sb_hd = cache_sb_k.shape
    _, _, n_mem, mem_heads, mem_hd = cache_mem_k.shape
    w = sb_heads * sb_hd
    d_ff = w_down.shape[1]
    assert mem_hd % V7X_LANES == 0 and V7X_LANES % sb_hd == 0 and d_ff % FFN_CHUNK == 0
    sb_scale = sb_hd ** -0.5
    assert math.frexp(sb_scale)[0] == 0.5, "the logit scale must be a power of two to fold into q exactly"

    tm_in = _pick_tile(seq, 512)
    tm_mix = _pick_tile(seq, 512)
    tm_ffn = _pick_tile(seq, 512)
    tq = _pick_tile(seq, 2 * V7X_LANES)
    tk = _pick_tile(tq, 2 * V7X_LANES)

    xp = x_prompt
    xs = x_sample.reshape(bs, d)
    conv_p, k_p, v_p, mk_p, mv_p, ffn_p = [], [], [], [], [], []
    conv_s, k_s, v_s, ffn_s = [], [], [], []
    for l in range(depth):
        row = lambda a: a[l].reshape(1, -1)
        w_a = w_in[l][:, :7 * w].astype(BF16)
        w_gate = jnp.moveaxis(w_in[l][:, 7 * w:].reshape(d, N_BRANCH, d), 1, 0).astype(BF16)
        w_br = w_branch[l].astype(BF16)
        w_out = w_o[l].astype(BF16)
        w_up_c = _chunk_cols(w_up[l], FFN_CHUNK).astype(BF16)
        w_dn_c = w_down[l].reshape(d_ff // FFN_CHUNK, FFN_CHUNK, d).astype(BF16)
        fcw_c = _chunk_cols(ffn_conv_w[l], FFN_CHUNK)
        fcb_c = _chunk_cols(ffn_conv_b[l].reshape(1, -1), FFN_CHUNK)
        bias = sb_bias[l]

        mk, mv, mkb, mvb = _memkv(mem_prompt, row(g_mem), w_mem_kv[l].astype(BF16))
        ya, q, k, v, kb, vb, yc, cs = _inproj_prompt(
            xp, row(g_mix_pre), w_a, conv_w[l], row(conv_b), mkb, mvb,
            sb_scale=sb_scale, mem_hd=mem_hd, tm=tm_in)
        yb = _sb_prompt(q, kb, vb, bias, hd=sb_hd, tq=tq, tk=tk)
        x1 = _mix(xp, ya, yb, yc, row(g_mix_pre), row(g_mix_post), w_gate, w_br, w_out, tm=tm_mix)
        xp, st = _ffn_prompt(x1, row(g_ffn_pre), row(g_ffn_post), w_up_c, fcw_c, fcb_c,
                             w_dn_c.reshape(d_ff, d), tm=tm_ffn)
        fs = jnp.moveaxis(st[:, :, V7X_SUBLANES - (CONV_WIDTH - 1):, :], 1, 2).reshape(bp, CONV_WIDTH - 1, -1)
        conv_p.append(cs)
        k_p.append(k.reshape(bp, seq, sb_heads, sb_hd))
        v_p.append(v.reshape(bp, seq, sb_heads, sb_hd))
        mk_p.append(mk.reshape(bp, n_mem, mem_heads, mem_hd))
        mv_p.append(mv.reshape(bp, n_mem, mem_heads, mem_hd))
        ffn_p.append(fs)

        st_c = state_conv[l]
        ya, q, k, v, mq, ccu = _inproj_sample(xs, row(g_mix_pre), w_a, conv_w[l], row(conv_b),
                                              st_c[:, 0], st_c[:, 1], sb_scale=sb_scale)
        bias_col = jnp.zeros((SAMPLE_HEAD_ROWS, 1), F32).at[:sb_heads, 0].set(bias)
        feature_major = lambda c: jnp.transpose(c[l], (0, 2, 3, 1)).reshape(n_pool, w, page)
        yb = _sb_sample(q, bias_col, feature_major(cache_sb_k), feature_major(cache_sb_v),
                        page_table, hd=sb_hd, pp=SAMPLE_PAGES_PER_STEP, rows=SAMPLE_HEAD_ROWS)
        yc = _mem_sample(mq, cache_mem_k[l].reshape(bs, n_mem, w), cache_mem_v[l].reshape(bs, n_mem, w),
                         mem_hd=mem_hd)
        x1 = _mix(xs[None], ya[None], yb[None], yc[None], row(g_mix_pre), row(g_mix_post),
                  w_gate, w_br, w_out, tm=bs)[0]
        st_f = state_ffn_conv[l]
        xs, up_c = _ffn_sample(x1, row(g_ffn_pre), row(g_ffn_post), w_up_c, fcw_c, fcb_c, w_dn_c,
                               _chunk_cols(st_f[:, 0], FFN_CHUNK), _chunk_cols(st_f[:, 1], FFN_CHUNK))
        conv_s.append(jnp.stack([st_c[:, 1], ccu], axis=1))
        k_s.append(k.reshape(bs, 1, sb_heads, sb_hd))
        v_s.append(v.reshape(bs, 1, sb_heads, sb_hd))
        ffn_s.append(jnp.stack([st_f[:, 1], _unchunk_cols(up_c)], axis=1))

    return (xp, xs.reshape(bs, 1, d), jnp.stack(conv_p), jnp.stack(k_p), jnp.stack(v_p), jnp.stack(mk_p),
            jnp.stack(mv_p), jnp.stack(ffn_p), jnp.stack(conv_s), jnp.stack(k_s), jnp.stack(v_s),
            jnp.stack(ffn_s))
```

```python
import functools
import math

import jax
import jax.numpy as jnp
from jax import lax
from jax.experimental import pallas as pl
from jax.experimental.pallas import tpu as pltpu

EPS = 1e-6
CONV_WIDTH = 3
N_BRANCH = 3
F32 = jnp.float32
BF16 = jnp.bfloat16

V7X_LANES = 128
V7X_SUBLANES = 8
V7X_VMEM_BYTES = 64 * 1024 * 1024
VMEM_LIMIT_BYTES = (V7X_VMEM_BYTES * 7) // 8


def _params(*semantics):
    return pltpu.CompilerParams(dimension_semantics=semantics, vmem_limit_bytes=VMEM_LIMIT_BYTES)


def _rms(x, g):
    ms = jnp.mean(x * x, axis=-1, keepdims=True)
    return (x * lax.rsqrt(ms + EPS)) * g


def _dot(a, b):
    return jnp.dot(a, b, preferred_element_type=F32)


def _dot_nt(a, b):
    return lax.dot_general(a, b, (((1,), (1,)), ((), ())), preferred_element_type=F32)


def _const_spec(shape):
    nd = len(shape)
    return pl.BlockSpec(shape, lambda *_: (0,) * nd, pipeline_mode=pl.Buffered(1))


def _pick_tile(n, want):
    t = min(n, want)
    assert n % t == 0, (n, t)
    return t


def _memkv_body(mem_ref, g_ref, w_ref, mk_ref, mv_ref, mkb_ref, mvb_ref):
    h = _rms(mem_ref[0], g_ref[...]).astype(BF16)
    kv = _dot(h, w_ref[...])
    w = kv.shape[1] // 2
    mk_ref[0] = kv[:, :w]
    mv_ref[0] = kv[:, w:]
    mkb_ref[0] = kv[:, :w].astype(BF16)
    mvb_ref[0] = kv[:, w:].astype(BF16)


def _memkv(mem, g_mem, w_mem_kv_bf):
    b, n_mem, d = mem.shape
    w = w_mem_kv_bf.shape[1] // 2
    blk = pl.BlockSpec((1, n_mem, w), lambda i: (i, 0, 0))
    return pl.pallas_call(
        _memkv_body,
        grid=(b,),
        in_specs=[pl.BlockSpec((1, n_mem, d), lambda i: (i, 0, 0)), _const_spec((1, d)),
                  _const_spec((d, 2 * w))],
        out_specs=[blk, blk, blk, blk],
        out_shape=[jax.ShapeDtypeStruct((b, n_mem, w), F32)] * 2
        + [jax.ShapeDtypeStruct((b, n_mem, w), BF16)] * 2,
        compiler_params=_params("arbitrary"),
        name="memkv",
    )(mem, g_mem, w_mem_kv_bf)


def _softmax_rows(s):
    m = jnp.max(s, axis=-1, keepdims=True)
    p = jnp.exp(s - m)
    return p / jnp.sum(p, axis=-1, keepdims=True)


def _inproj_prompt_body(x_ref, g_ref, w_ref, cw_ref, cb_ref, mk_ref, mv_ref,
                        ya_ref, q_ref, k_ref, v_ref, kb_ref, vb_ref, yc_ref, cs_ref,
                        s_scr, *, tm, w, sb_scale, mem_hd):
    ti = pl.program_id(1)
    h = _rms(x_ref[0], g_ref[...]).astype(BF16)

    def grp(i):
        return _dot(h, w_ref[:, i * w:(i + 1) * w])

    ccu = grp(1) * grp(2)
    @pl.when(ti == 0)
    def _():
        s_scr[tm:tm + V7X_SUBLANES, :] = jnp.zeros((V7X_SUBLANES, w), F32)

    s_scr[0:V7X_SUBLANES, :] = s_scr[tm:tm + V7X_SUBLANES, :]
    s_scr[V7X_SUBLANES:, :] = ccu
    r1 = s_scr[pl.ds(V7X_SUBLANES - 1, tm), :]
    r2 = s_scr[pl.ds(V7X_SUBLANES - 2, tm), :]
    cw = cw_ref[...]
    conv = cb_ref[...] + cw[2:3] * ccu
    conv = conv + cw[0:1] * r2
    conv = conv + cw[1:2] * r1
    ya_ref[0] = (grp(0) * conv).astype(BF16)
    cs_ref[0] = s_scr[pl.ds(tm + V7X_SUBLANES - (CONV_WIDTH - 1), CONV_WIDTH - 1), :]

    q_ref[0] = (grp(3) * sb_scale).astype(BF16)
    k = grp(4)
    k_ref[0] = k
    kb_ref[0] = k.astype(BF16)
    v = grp(5)
    v_ref[0] = v
    vb_ref[0] = v.astype(BF16)

    mq = grp(6)
    outs = []
    for hd in range(w // mem_hd):
        sl = slice(hd * mem_hd, (hd + 1) * mem_hd)
        s = _dot_nt(mq[:, sl].astype(BF16), mk_ref[0, :, sl]) * (mem_hd ** -0.5)
        p = _softmax_rows(s)
        outs.append(_dot(p.astype(BF16), mv_ref[0, :, sl]))
    yc_ref[0] = jnp.concatenate(outs, axis=1).astype(BF16)


def _inproj_prompt(x, g, w_a, conv_w, conv_b, mkb, mvb, *, sb_scale, mem_hd, tm):
    b, t, d = x.shape
    w = conv_w.shape[1]
    n_mem = mkb.shape[1]
    row = lambda dt: jax.ShapeDtypeStruct((b, t, w), dt)
    blk = pl.BlockSpec((1, tm, w), lambda i, j: (i, j, 0))
    body = functools.partial(_inproj_prompt_body, tm=tm, w=w, sb_scale=sb_scale, mem_hd=mem_hd)
    return pl.pallas_call(
        body,
        grid=(b, t // tm),
        in_specs=[pl.BlockSpec((1, tm, d), lambda i, j: (i, j, 0)), _const_spec((1, d)),
                  _const_spec(w_a.shape), _const_spec(conv_w.shape), _const_spec((1, w)),
                  pl.BlockSpec((1, n_mem, w), lambda i, j: (i, 0, 0)),
                  pl.BlockSpec((1, n_mem, w), lambda i, j: (i, 0, 0))],
        out_specs=[blk] * 7 + [pl.BlockSpec((1, CONV_WIDTH - 1, w), lambda i, j: (i, 0, 0))],
        out_shape=[row(BF16), row(BF16), row(F32), row(F32), row(BF16), row(BF16), row(BF16),
                   jax.ShapeDtypeStruct((b, CONV_WIDTH - 1, w), F32)],
        scratch_shapes=[pltpu.VMEM((tm + V7X_SUBLANES, w), F32)],
        compiler_params=_params("arbitrary", "arbitrary"),
        name="inproj_prompt",
    )(x, g, w_a, conv_w, conv_b, mkb, mvb)


def _inproj_sample_body(x_ref, g_ref, w_ref, cw_ref, cb_ref, st0_ref, st1_ref,
                        ya_ref, q_ref, k_ref, v_ref, mq_ref, ccu_ref, *, w, sb_scale):
    h = _rms(x_ref[...], g_ref[...]).astype(BF16)

    def grp(i):
        return _dot(h, w_ref[:, i * w:(i + 1) * w])

    ccu = grp(1) * grp(2)
    cw = cw_ref[...]
    conv = cb_ref[...] + cw[2:3] * ccu
    conv = conv + cw[0:1] * st0_ref[...]
    conv = conv + cw[1:2] * st1_ref[...]
    ya_ref[...] = (grp(0) * conv).astype(BF16)
    ccu_ref[...] = ccu
    q_ref[...] = grp(3) * sb_scale
    k_ref[...] = grp(4)
    v_ref[...] = grp(5)
    mq_ref[...] = grp(6)


def _inproj_sample(x, g, w_a, conv_w, conv_b, st0, st1, *, sb_scale):
    n, d = x.shape
    w = conv_w.shape[1]
    row = lambda dt: jax.ShapeDtypeStruct((n, w), dt)
    body = functools.partial(_inproj_sample_body, w=w, sb_scale=sb_scale)
    return pl.pallas_call(
        body,
        grid=(1,),
        in_specs=[_const_spec((n, d)), _const_spec((1, d)), _const_spec(w_a.shape),
                  _const_spec(conv_w.shape), _const_spec((1, w)), _const_spec((n, w)),
                  _const_spec((n, w))],
        out_specs=[_const_spec((n, w))] * 6,
        out_shape=[row(BF16), row(F32), row(F32), row(F32), row(F32), row(F32)],
        compiler_params=_params("arbitrary"),
        name="inproj_sample",
    )(x, g, w_a, conv_w, conv_b, st0, st1)


_SIGN_BIT = -2 ** 31


def _softplus(z):
    neg_abs = lax.bitcast_convert_type(lax.bitcast_convert_type(z, jnp.int32) | _SIGN_BIT, F32)
    return jnp.maximum(z, 0.0) + jnp.log(1.0 + jnp.exp(neg_abs))


def _sb_log_weights(z, causal, u):
    sp = _softplus(z)
    if causal is not None:
        sp = jnp.where(causal, sp, 0.0)
    incl = _dot(sp.astype(BF16), u)
    return z - incl, jnp.broadcast_to(incl[:, :1], (z.shape[0], V7X_LANES))


def _cumsum_weights(n):
    row = lax.broadcasted_iota(jnp.int32, (n, n), 0)
    col = lax.broadcasted_iota(jnp.int32, (n, n), 1)
    return jnp.where(row >= col, 1.0, 0.0).astype(BF16)


def _lanes(x, n):
    return jnp.concatenate([x] * (n // V7X_LANES), axis=1)


def _sb_prompt_body(bias_ref, q_ref, k_ref, v_ref, o_ref, qs_scr, acc_scr, carry_scr,
                    lwa_scr, lwb_scr, resta_scr, restb_scr, *, tq, tk, hd):
    hp = pl.program_id(1)
    i = pl.program_id(2)
    hpb = V7X_LANES // hd
    ratio = tq // tk
    q2 = q_ref[0]
    lane_head = lax.broadcasted_iota(jnp.int32, (tq, V7X_LANES), 1) // hd
    for h in range(hpb):
        qs_scr[h * tq:(h + 1) * tq, :] = jnp.where(lane_head == h, q2, jnp.zeros_like(q2))
    u = _cumsum_weights(tk)
    acc_scr[...] = jnp.zeros_like(acc_scr)
    carry_scr[...] = jnp.zeros_like(carry_scr)

    def log_weights(j, mask):
        ks = pl.multiple_of(j * tk, tk)
        mm = _dot_nt(qs_scr[...], k_ref[0, pl.ds(ks, tk), :])
        z = jnp.concatenate([mm[h * tq:(h + 1) * tq] + bias_ref[hp * hpb + h] for h in range(hpb)], axis=0)
        return _sb_log_weights(z, mask, u)

    def values(j):
        return v_ref[0, pl.ds(pl.multiple_of(j * tk, tk), tk), :]

    def chain(j_top, masks):
        n = len(masks)
        parts = [log_weights(j_top - t, masks[t]) for t in range(n)]
        carry = carry_scr[...]
        weights = []
        for t, (lw, rest) in enumerate(parts):
            a = jnp.exp(lw - _lanes(carry, tk))
            if masks[t] is not None:
                a = jnp.where(masks[t], a, 0.0)
            weights.append(a.astype(BF16))
            carry = carry + rest
        acc_scr[...] += _dot(jnp.concatenate(weights, axis=1),
                             jnp.concatenate([values(j_top - t) for t in range(n)], axis=0))
        carry_scr[...] = carry

    row = lax.broadcasted_iota(jnp.int32, (tq, tk), 0)
    col = lax.broadcasted_iota(jnp.int32, (tq, tk), 1)
    diag_masks = [jnp.concatenate([col + d * tk < row] * hpb, axis=0) for d in range(ratio - 1, -1, -1)]
    n_full = ratio * i
    odd = lax.bitwise_and(n_full, 1)
    n_pairs = lax.shift_right_logical(n_full, 1)
    top = n_full - 1 - odd

    def front(p, lw_ref, rest_ref):
        for t in range(2):
            j = jnp.maximum(top - 2 * p - t, 0)
            lw, rest = log_weights(j, None)
            lw_ref[t] = lw
            rest_ref[t] = rest

    def back(p, lw_ref, rest_ref):
        j = top - 2 * p
        carry = carry_scr[...]
        rest1 = rest_ref[0]
        a1 = jnp.exp(lw_ref[0] - _lanes(carry, tk)).astype(BF16)
        a0 = jnp.exp(lw_ref[1] - _lanes(carry + rest1, tk)).astype(BF16)
        acc_scr[...] += _dot(jnp.concatenate([a1, a0], axis=1),
                             jnp.concatenate([values(j), values(j - 1)], axis=0))
        carry_scr[...] = carry + rest1 + rest_ref[1]

    front(0, lwa_scr, resta_scr)
    chain(ratio * i + ratio - 1, diag_masks)

    @pl.when(odd == 1)
    def _():
        chain(n_full - 1, [None])

    def loop_body(n, c):
        p = 2 * n
        front(p + 1, lwb_scr, restb_scr)
        back(p, lwa_scr, resta_scr)

        @pl.when(p + 1 < n_pairs)
        def _():
            front(p + 2, lwa_scr, resta_scr)
            back(p + 1, lwb_scr, restb_scr)

        return c

    lax.fori_loop(0, lax.shift_right_logical(n_pairs + 1, 1), loop_body, 0)

    out = acc_scr[(hpb - 1) * tq:, :]
    for h in range(hpb - 2, -1, -1):
        out = jnp.where(lane_head == h, acc_scr[h * tq:(h + 1) * tq, :], out)
    o_ref[0] = out.astype(BF16)


def _sb_prompt(q, k, v, bias, *, hd, tq, tk):
    b, t, w = q.shape
    assert tq % tk == 0 and t % tq == 0
    body = functools.partial(_sb_prompt_body, tq=tq, tk=tk, hd=hd)
    hpb = V7X_LANES // hd
    pair = (2, hpb * tq)
    grid_spec = pltpu.PrefetchScalarGridSpec(
        num_scalar_prefetch=1,
        grid=(b, w // V7X_LANES, t // tq),
        in_specs=[pl.BlockSpec((1, tq, V7X_LANES), lambda bi, hp, i, _: (bi, i, hp)),
                  pl.BlockSpec((1, t, V7X_LANES), lambda bi, hp, i, _: (bi, 0, hp)),
                  pl.BlockSpec((1, t, V7X_LANES), lambda bi, hp, i, _: (bi, 0, hp))],
        out_specs=pl.BlockSpec((1, tq, V7X_LANES), lambda bi, hp, i, _: (bi, i, hp)),
        scratch_shapes=[pltpu.VMEM((hpb * tq, V7X_LANES), BF16), pltpu.VMEM((hpb * tq, V7X_LANES), F32),
                        pltpu.VMEM((hpb * tq, V7X_LANES), F32),
                        pltpu.VMEM(pair + (tk,), F32), pltpu.VMEM(pair + (tk,), F32),
                        pltpu.VMEM(pair + (V7X_LANES,), F32), pltpu.VMEM(pair + (V7X_LANES,), F32)],
    )
    return pl.pallas_call(
        body,
        grid_spec=grid_spec,
        out_shape=jax.ShapeDtypeStruct((b, t, w), BF16),
        compiler_params=_params("arbitrary", "arbitrary", "arbitrary"),
        name="sb_prompt",
    )(bias, q, k, v)


def _sb_sample_body(pt_ref, q_ref, bias_ref, *refs, pp, hd, rows):
    k_refs = refs[:pp]
    v_refs = refs[pp:2 * pp]
    o_ref = refs[2 * pp]
    acc_scr, carry_scr = refs[2 * pp + 1:]
    g = pl.program_id(1)
    w = q_ref.shape[-1]
    page = k_refs[0].shape[2]

    @pl.when(g == 0)
    def _():
        acc_scr[...] = jnp.zeros_like(acc_scr)
        carry_scr[...] = jnp.zeros_like(carry_scr)

    row_head = lax.broadcasted_iota(jnp.int32, (rows, w), 0)
    lane_head = lax.broadcasted_iota(jnp.int32, (rows, w), 1) // hd
    on_diag = row_head == lane_head
    qrow = jnp.broadcast_to(q_ref[0], (rows, w))
    qbd = jnp.where(on_diag, qrow, 0.0).astype(BF16)
    u = _cumsum_weights(page)

    kt = jnp.concatenate([k_refs[r][0] for r in range(pp)], axis=1).astype(BF16)
    z = _dot(qbd, kt) + bias_ref[...]
    z_st = jnp.concatenate([z[:, r * page:(r + 1) * page] for r in range(pp)], axis=0)
    lw_st, rest_st = _sb_log_weights(z_st, None, u)
    after = carry_scr[...]
    afters = [None] * pp
    for r in range(pp - 1, -1, -1):
        afters[r] = after
        after = after + rest_st[r * rows:(r + 1) * rows]
    carry_scr[...] = after
    a_st = jnp.exp(lw_st - _lanes(jnp.concatenate(afters, axis=0), page)).astype(BF16)
    a = jnp.concatenate([a_st[r * rows:(r + 1) * rows] for r in range(pp)], axis=1)
    vt = jnp.concatenate([v_refs[r][0] for r in range(pp)], axis=1).astype(BF16)
    acc_scr[...] += _dot_nt(a, vt)

    @pl.when(g == pl.num_programs(1) - 1)
    def _():
        picked = jnp.where(on_diag, acc_scr[...], 0.0)
        o_ref[0] = jnp.sum(picked, axis=0, keepdims=True).astype(BF16)


def _sb_sample(q, bias_col, cache_kt, cache_vt, page_table, *, hd, pp, rows):
    n, w = q.shape
    n_pages = page_table.shape[1]
    page = cache_kt.shape[2]
    assert n_pages % pp == 0
    ng = n_pages // pp

    def page_spec(r):
        return pl.BlockSpec((1, w, page), lambda bi, g, pt: (pt[bi, (ng - 1 - g) * pp + r], 0, 0))

    body = functools.partial(_sb_sample_body, pp=pp, hd=hd, rows=rows)
    grid_spec = pltpu.PrefetchScalarGridSpec(
        num_scalar_prefetch=1,
        grid=(n, ng),
        in_specs=[pl.BlockSpec((1, 1, w), lambda bi, g, pt: (bi, 0, 0)),
                  pl.BlockSpec((rows, 1), lambda bi, g, pt: (0, 0))]
        + [page_spec(r) for r in range(pp)] * 2,
        out_specs=pl.BlockSpec((1, 1, w), lambda bi, g, pt: (bi, 0, 0)),
        scratch_shapes=[pltpu.VMEM((rows, w), F32), pltpu.VMEM((rows, V7X_LANES), F32)],
    )
    out = pl.pallas_call(
        body,
        grid_spec=grid_spec,
        out_shape=jax.ShapeDtypeStruct((n, 1, w), BF16),
        compiler_params=_params("arbitrary", "arbitrary"),
        name="sb_sample",
    )(page_table, q.reshape(n, 1, w), bias_col, *([cache_kt] * pp), *([cache_vt] * pp))
    return out.reshape(n, w)


def _mem_sample_body(q_ref, mk_ref, mv_ref, o_ref, *, heads):
    g = V7X_SUBLANES
    rows, hd = mk_ref.shape[1:]
    k3 = mk_ref[0].reshape(rows // g, g, hd)
    v3 = mv_ref[0].reshape(rows // g, g, hd)

    def over_tokens(x, op, reduce):
        y = reduce(x, axis=0)
        out = y
        for rep in range(1, g // heads):
            out = op(out, pltpu.roll(y, rep * heads, axis=0))
        return out

    s = jnp.sum(k3 * q_ref[0], axis=-1, keepdims=True) * (hd ** -0.5)
    s = jnp.broadcast_to(s, k3.shape)
    p = jnp.exp(s - over_tokens(s, jnp.maximum, jnp.max))
    p = p / over_tokens(p, jnp.add, jnp.sum)
    o_ref[0] = over_tokens(p * v3, jnp.add, jnp.sum)[:heads].astype(BF16)


def _mem_sample(mq, mem_k, mem_v):
    n, heads, hd = mq.shape
    rows = mem_k.shape[1]
    assert V7X_SUBLANES % heads == 0
    q8 = jnp.tile(mq, (1, V7X_SUBLANES // heads, 1))
    mem = pl.BlockSpec((1, rows, hd), lambda i: (i, 0, 0))
    return pl.pallas_call(
        functools.partial(_mem_sample_body, heads=heads),
        grid=(n,),
        in_specs=[pl.BlockSpec((1, V7X_SUBLANES, hd), lambda i: (i, 0, 0)), mem, mem],
        out_specs=pl.BlockSpec((1, heads, hd), lambda i: (i, 0, 0)),
        out_shape=jax.ShapeDtypeStruct((n, heads, hd), BF16),
        compiler_params=_params("arbitrary"),
        name="mem_sample",
    )(q8, mem_k, mem_v)


def _mix_body(x_ref, ya_ref, yb_ref, yc_ref, gpre_ref, gpost_ref, wg_ref, wb_ref, wo_ref, o_ref):
    x = x_ref[0]
    h = _rms(x, gpre_ref[...]).astype(BF16)
    ys = (ya_ref[0], yb_ref[0], yc_ref[0])
    merged = None
    for n in range(N_BRANCH):
        gate = 1.0 / (1.0 + jnp.exp(-_dot(h, wg_ref[n])))
        term = gate * _dot(ys[n], wb_ref[n])
        merged = term if merged is None else merged + term
    mixed = _dot(merged.astype(BF16), wo_ref[...])
    o_ref[0] = x + _rms(mixed, gpost_ref[...])


def _mix(x, ya, yb, yc, g_pre, g_post, w_gate, w_branch, w_o, *, tm):
    b, t, d = x.shape
    w = ya.shape[-1]
    xblk = pl.BlockSpec((1, tm, d), lambda i, j: (i, j, 0))
    yblk = pl.BlockSpec((1, tm, w), lambda i, j: (i, j, 0))
    return pl.pallas_call(
        _mix_body,
        grid=(b, t // tm),
        in_specs=[xblk, yblk, yblk, yblk, _const_spec((1, d)), _const_spec((1, d)),
                  _const_spec(w_gate.shape), _const_spec(w_branch.shape), _const_spec(w_o.shape)],
        out_specs=xblk,
        out_shape=jax.ShapeDtypeStruct((b, t, d), F32),
        compiler_params=_params("arbitrary", "arbitrary"),
        name="mix",
    )(x, ya, yb, yc, g_pre, g_post, w_gate, w_branch, w_o)


def _gelu_tanh(x):
    return x * (0.5 * (1.0 + jnp.tanh((2.0 / jnp.pi) ** 0.5 * (x + 0.044715 * (x * x * x)))))


def _ffn_prompt_body(x_ref, gpre_ref, gpost_ref, wup_ref, fcw_ref, fcb_ref, wdn_ref,
                     o_ref, st_ref, hn_scr, act_scr, carry_scr, s_scr, *, tm, nc):
    ti = pl.program_id(1)
    width = wup_ref.shape[2]

    @pl.when(ti == 0)
    def _():
        carry_scr[...] = jnp.zeros_like(carry_scr)

    x = x_ref[0]
    hn_scr[...] = _rms(x, gpre_ref[...]).astype(BF16)

    def conv_chunk(idx, slot):
        u = _dot(hn_scr[...], wup_ref[idx])
        s_scr[slot, 0:V7X_SUBLANES, :] = carry_scr[idx]
        s_scr[slot, V7X_SUBLANES:, :] = u
        last = u[tm - V7X_SUBLANES:, :]
        carry_scr[idx] = last
        st_ref[0, idx] = last
        r1 = s_scr[slot, pl.ds(V7X_SUBLANES - 1, tm), :]
        r2 = s_scr[slot, pl.ds(V7X_SUBLANES - 2, tm), :]
        cw = fcw_ref[idx]
        y = fcb_ref[idx] + cw[2:3] * u
        y = y + cw[0:1] * r2
        return y + cw[1:2] * r1

    for c in range(nc):
        a = conv_chunk(c, 2 * (c % 2))
        bv = conv_chunk(c + nc, 2 * (c % 2) + 1)
        act_scr[:, c * width:(c + 1) * width] = (_gelu_tanh(a) * bv).astype(BF16)
    f = _dot(act_scr[...], wdn_ref[...])
    o_ref[0] = x + _rms(f, gpost_ref[...])


def _ffn_prompt(x, g_pre, g_post, w_up_c, fcw_c, fcb_c, w_dn, *, tm):
    b, t, d = x.shape
    nc2, _, cw = w_up_c.shape
    nc = nc2 // 2
    xblk = pl.BlockSpec((1, tm, d), lambda i, j: (i, j, 0))
    body = functools.partial(_ffn_prompt_body, tm=tm, nc=nc)
    return pl.pallas_call(
        body,
        grid=(b, t // tm),
        in_specs=[xblk, _const_spec((1, d)), _const_spec((1, d)), _const_spec(w_up_c.shape),
                  _const_spec(fcw_c.shape), _const_spec(fcb_c.shape), _const_spec(w_dn.shape)],
        out_specs=[xblk, pl.BlockSpec((1, nc2, V7X_SUBLANES, cw), lambda i, j: (i, 0, 0, 0))],
        out_shape=[jax.ShapeDtypeStruct((b, t, d), F32),
                   jax.ShapeDtypeStruct((b, nc2, V7X_SUBLANES, cw), F32)],
        scratch_shapes=[pltpu.VMEM((tm, d), BF16), pltpu.VMEM((tm, nc * cw), BF16),
                        pltpu.VMEM((nc2, V7X_SUBLANES, cw), F32),
                        pltpu.VMEM((4, tm + V7X_SUBLANES, cw), F32)],
        compiler_params=_params("arbitrary", "arbitrary"),
        name="ffn_prompt",
    )(x, g_pre, g_post, w_up_c, fcw_c, fcb_c, w_dn)


def _ffn_sample_body(x_ref, gpre_ref, gpost_ref, wup_ref, fcw_ref, fcb_ref, wdn_ref, st0_ref, st1_ref,
                     o_ref, up_ref, hn_scr, f_scr, *, nc):
    x = x_ref[...]
    hn_scr[...] = _rms(x, gpre_ref[...]).astype(BF16)
    f_scr[...] = jnp.zeros_like(f_scr)

    def conv_chunk(idx):
        u = _dot(hn_scr[...], wup_ref[idx])
        up_ref[idx] = u
        cw = fcw_ref[idx]
        y = fcb_ref[idx] + cw[2:3] * u
        y = y + cw[0:1] * st0_ref[idx]
        return y + cw[1:2] * st1_ref[idx]

    def chunk(c, carry):
        act = (_gelu_tanh(conv_chunk(c)) * conv_chunk(c + nc)).astype(BF16)
        f_scr[...] += _dot(act, wdn_ref[c])
        return carry

    lax.fori_loop(0, nc, chunk, 0)
    o_ref[...] = x + _rms(f_scr[...], gpost_ref[...])


def _ffn_sample(x, g_pre, g_post, w_up_c, fcw_c, fcb_c, w_dn_c, st0_c, st1_c):
    n, d = x.shape
    nc2, _, cw = w_up_c.shape
    body = functools.partial(_ffn_sample_body, nc=nc2 // 2)
    return pl.pallas_call(
        body,
        grid=(1,),
        in_specs=[_const_spec((n, d)), _const_spec((1, d)), _const_spec((1, d)), _const_spec(w_up_c.shape),
                  _const_spec(fcw_c.shape), _const_spec(fcb_c.shape), _const_spec(w_dn_c.shape),
                  _const_spec(st0_c.shape), _const_spec(st1_c.shape)],
        out_specs=[_const_spec((n, d)), _const_spec((nc2, n, cw))],
        out_shape=[jax.ShapeDtypeStruct((n, d), F32), jax.ShapeDtypeStruct((nc2, n, cw), F32)],
        scratch_shapes=[pltpu.VMEM((n, d), BF16), pltpu.VMEM((n, d), F32)],
        compiler_params=_params("arbitrary"),
        name="ffn_sample",
    )(x, g_pre, g_post, w_up_c, fcw_c, fcb_c, w_dn_c, st0_c, st1_c)


FFN_CHUNK = 2 * V7X_LANES
SAMPLE_PAGES_PER_STEP = 16
SAMPLE_HEAD_ROWS = 2 * V7X_SUBLANES


def _chunk_cols(a, cw):
    n = a.shape[-1]
    a = a.reshape(a.shape[:-1] + (n // cw, cw))
    return jnp.moveaxis(a, -2, 0)


def _unchunk_cols(a):
    nc, r, cw = a.shape
    return jnp.moveaxis(a, 0, 1).reshape(r, nc * cw)


def kernel(x_prompt, x_sample, cache_sb_k, cache_sb_v, cache_mem_k, cache_mem_v, state_conv, state_ffn_conv,
           page_table, mem_prompt, g_mix_pre, g_mix_post, g_ffn_pre, g_ffn_post, g_mem, w_in, conv_w, conv_b,
           sb_bias, w_mem_kv, w_branch, w_o, w_up, ffn_conv_w, ffn_conv_b, w_down):
    depth = w_in.shape[0]
    bp, seq, d = x_prompt.shape
    bs, dec_seq, _ = x_sample.shape
    assert dec_seq == 1, "the sample group advances one token per step"
    _, n_pool, page, sb_heads, sb_hd = cache_sb_k.shape
    _, _, n_mem, mem_heads, mem_hd = cache_mem_k.shape
    w = sb_heads * sb_hd
    d_ff = w_down.shape[1]
    assert mem_hd % V7X_LANES == 0 and V7X_LANES % sb_hd == 0 and d_ff % FFN_CHUNK == 0
    sb_scale = sb_hd ** -0.5
    assert math.frexp(sb_scale)[0] == 0.5, "the logit scale must be a power of two to fold into q exactly"

    tm_in = _pick_tile(seq, 512)
    tm_mix = _pick_tile(seq, 512)
    tm_ffn = _pick_tile(seq, 512)
    tq = _pick_tile(seq, 2 * V7X_LANES)
    tk = _pick_tile(tq, 2 * V7X_LANES)

    xp = x_prompt
    xs = x_sample.reshape(bs, d)
    conv_p, k_p, v_p, mk_p, mv_p, ffn_p = [], [], [], [], [], []
    conv_s, k_s, v_s, ffn_s = [], [], [], []
    for l in range(depth):
        row = lambda a: a[l].reshape(1, -1)
        w_a = w_in[l][:, :7 * w].astype(BF16)
        w_gate = jnp.moveaxis(w_in[l][:, 7 * w:].reshape(d, N_BRANCH, d), 1, 0).astype(BF16)
        w_br = w_branch[l].astype(BF16)
        w_out = w_o[l].astype(BF16)
        w_up_c = _chunk_cols(w_up[l], FFN_CHUNK).astype(BF16)
        w_dn_c = w_down[l].reshape(d_ff // FFN_CHUNK, FFN_CHUNK, d).astype(BF16)
        fcw_c = _chunk_cols(ffn_conv_w[l], FFN_CHUNK)
        fcb_c = _chunk_cols(ffn_conv_b[l].reshape(1, -1), FFN_CHUNK)
        bias = sb_bias[l]

        mk, mv, mkb, mvb = _memkv(mem_prompt, row(g_mem), w_mem_kv[l].astype(BF16))
        ya, q, k, v, kb, vb, yc, cs = _inproj_prompt(
            xp, row(g_mix_pre), w_a, conv_w[l], row(conv_b), mkb, mvb,
            sb_scale=sb_scale, mem_hd=mem_hd, tm=tm_in)
        yb = _sb_prompt(q, kb, vb, bias, hd=sb_hd, tq=tq, tk=tk)
        x1 = _mix(xp, ya, yb, yc, row(g_mix_pre), row(g_mix_post), w_gate, w_br, w_out, tm=tm_mix)
        xp, st = _ffn_prompt(x1, row(g_ffn_pre), row(g_ffn_post), w_up_c, fcw_c, fcb_c,
                             w_dn_c.reshape(d_ff, d), tm=tm_ffn)
        fs = jnp.moveaxis(st[:, :, V7X_SUBLANES - (CONV_WIDTH - 1):, :], 1, 2).reshape(bp, CONV_WIDTH - 1, -1)
        conv_p.append(cs)
        k_p.append(k.reshape(bp, seq, sb_heads, sb_hd))
        v_p.append(v.reshape(bp, seq, sb_heads, sb_hd))
        mk_p.append(mk.reshape(bp, n_mem, mem_heads, mem_hd))
        mv_p.append(mv.reshape(bp, n_mem, mem_heads, mem_hd))
        ffn_p.append(fs)

        st_c = state_conv[l]
        ya, q, k, v, mq, ccu = _inproj_sample(xs, row(g_mix_pre), w_a, conv_w[l], row(conv_b),
                                              st_c[:, 0], st_c[:, 1], sb_scale=sb_scale)
        bias_col = jnp.zeros((SAMPLE_HEAD_ROWS, 1), F32).at[:sb_heads, 0].set(bias)
        feature_major = lambda c: jnp.transpose(c[l], (0, 2, 3, 1)).reshape(n_pool, w, page)
        yb = _sb_sample(q, bias_col, feature_major(cache_sb_k), feature_major(cache_sb_v),
                        page_table, hd=sb_hd, pp=SAMPLE_PAGES_PER_STEP, rows=SAMPLE_HEAD_ROWS)
        yc = _mem_sample(mq.reshape(bs, mem_heads, mem_hd),
                         cache_mem_k[l].reshape(bs, n_mem * mem_heads, mem_hd),
                         cache_mem_v[l].reshape(bs, n_mem * mem_heads, mem_hd)).reshape(bs, w)
        x1 = _mix(xs[None], ya[None], yb[None], yc[None], row(g_mix_pre), row(g_mix_post),
                  w_gate, w_br, w_out, tm=bs)[0]
        st_f = state_ffn_conv[l]
        xs, up_c = _ffn_sample(x1, row(g_ffn_pre), row(g_ffn_post), w_up_c, fcw_c, fcb_c, w_dn_c,
                               _chunk_cols(st_f[:, 0], FFN_CHUNK), _chunk_cols(st_f[:, 1], FFN_CHUNK))
        conv_s.append(jnp.stack([st_c[:, 1], ccu], axis=1))
        k_s.append(k.reshape(bs, 1, sb_heads, sb_hd))
        v_s.append(v.reshape(bs, 1, sb_heads, sb_hd))
        ffn_s.append(jnp.stack([st_f[:, 1], _unchunk_cols(up_c)], axis=1))

    return (xp, xs.reshape(bs, 1, d), jnp.stack(conv_p), jnp.stack(k_p), jnp.stack(v_p), jnp.stack(mk_p),
            jnp.stack(mv_p), jnp.stack(ffn_p), jnp.stack(conv_s), jnp.stack(k_s), jnp.stack(v_s),
            jnp.stack(ffn_s))
```

```python
import functools
import math

import jax
import jax.numpy as jnp
from jax import lax
from jax.experimental import pallas as pl
from jax.experimental.pallas import tpu as pltpu

EPS = 1e-6
CONV_WIDTH = 3
N_BRANCH = 3
F32 = jnp.float32
BF16 = jnp.bfloat16

V7X_LANES = 128
V7X_SUBLANES = 8
V7X_VMEM_BYTES = 64 * 1024 * 1024
VMEM_LIMIT_BYTES = (V7X_VMEM_BYTES * 7) // 8


def _params(*semantics):
    return pltpu.CompilerParams(dimension_semantics=semantics, vmem_limit_bytes=VMEM_LIMIT_BYTES)


def _rms(x, g):
    ms = jnp.mean(x * x, axis=-1, keepdims=True)
    return (x * lax.rsqrt(ms + EPS)) * g


def _dot(a, b):
    return jnp.dot(a, b, preferred_element_type=F32)


def _dot_nt(a, b):
    return lax.dot_general(a, b, (((1,), (1,)), ((), ())), preferred_element_type=F32)


def _const_spec(shape):
    nd = len(shape)
    return pl.BlockSpec(shape, lambda *_: (0,) * nd, pipeline_mode=pl.Buffered(1))


def _pick_tile(n, want):
    t = min(n, want)
    assert n % t == 0, (n, t)
    return t


def _memkv_body(mem_ref, g_ref, w_ref, mk_ref, mv_ref, mkb_ref, mvb_ref):
    h = _rms(mem_ref[0], g_ref[...]).astype(BF16)
    kv = _dot(h, w_ref[...])
    w = kv.shape[1] // 2
    mk_ref[0] = kv[:, :w]
    mv_ref[0] = kv[:, w:]
    mkb_ref[0] = kv[:, :w].astype(BF16)
    mvb_ref[0] = kv[:, w:].astype(BF16)


def _memkv(mem, g_mem, w_mem_kv_bf):
    b, n_mem, d = mem.shape
    w = w_mem_kv_bf.shape[1] // 2
    blk = pl.BlockSpec((1, n_mem, w), lambda i: (i, 0, 0))
    return pl.pallas_call(
        _memkv_body,
        grid=(b,),
        in_specs=[pl.BlockSpec((1, n_mem, d), lambda i: (i, 0, 0)), _const_spec((1, d)),
                  _const_spec((d, 2 * w))],
        out_specs=[blk, blk, blk, blk],
        out_shape=[jax.ShapeDtypeStruct((b, n_mem, w), F32)] * 2
        + [jax.ShapeDtypeStruct((b, n_mem, w), BF16)] * 2,
        compiler_params=_params("arbitrary"),
        name="memkv",
    )(mem, g_mem, w_mem_kv_bf)


def _softmax_rows(s):
    m = jnp.max(s, axis=-1, keepdims=True)
    p = jnp.exp(s - m)
    return p / jnp.sum(p, axis=-1, keepdims=True)


def _inproj_prompt_body(x_ref, g_ref, w_ref, cw_ref, cb_ref, mk_ref, mv_ref,
                        ya_ref, q_ref, k_ref, v_ref, kb_ref, vb_ref, yc_ref, cs_ref,
                        s_scr, *, tm, w, sb_scale, mem_hd):
    ti = pl.program_id(1)
    h = _rms(x_ref[0], g_ref[...]).astype(BF16)

    def grp(i):
        return _dot(h, w_ref[:, i * w:(i + 1) * w])

    ccu = grp(1) * grp(2)
    @pl.when(ti == 0)
    def _():
        s_scr[tm:tm + V7X_SUBLANES, :] = jnp.zeros((V7X_SUBLANES, w), F32)

    s_scr[0:V7X_SUBLANES, :] = s_scr[tm:tm + V7X_SUBLANES, :]
    s_scr[V7X_SUBLANES:, :] = ccu
    r1 = s_scr[pl.ds(V7X_SUBLANES - 1, tm), :]
    r2 = s_scr[pl.ds(V7X_SUBLANES - 2, tm), :]
    cw = cw_ref[...]
    conv = cb_ref[...] + cw[2:3] * ccu
    conv = conv + cw[0:1] * r2
    conv = conv + cw[1:2] * r1
    ya_ref[0] = (grp(0) * conv).astype(BF16)
    cs_ref[0] = s_scr[pl.ds(tm + V7X_SUBLANES - (CONV_WIDTH - 1), CONV_WIDTH - 1), :]

    q_ref[0] = (grp(3) * sb_scale).astype(BF16)
    k = grp(4)
    k_ref[0] = k
    kb_ref[0] = k.astype(BF16)
    v = grp(5)
    v_ref[0] = v
    vb_ref[0] = v.astype(BF16)

    mq = grp(6)
    outs = []
    for hd in range(w // mem_hd):
        sl = slice(hd * mem_hd, (hd + 1) * mem_hd)
        s = _dot_nt(mq[:, sl].astype(BF16), mk_ref[0, :, sl]) * (mem_hd ** -0.5)
        p = _softmax_rows(s)
        outs.append(_dot(p.astype(BF16), mv_ref[0, :, sl]))
    yc_ref[0] = jnp.concatenate(outs, axis=1).astype(BF16)


def _inproj_prompt(x, g, w_a, conv_w, conv_b, mkb, mvb, *, sb_scale, mem_hd, tm):
    b, t, d = x.shape
    w = conv_w.shape[1]
    n_mem = mkb.shape[1]
    row = lambda dt: jax.ShapeDtypeStruct((b, t, w), dt)
    blk = pl.BlockSpec((1, tm, w), lambda i, j: (i, j, 0))
    body = functools.partial(_inproj_prompt_body, tm=tm, w=w, sb_scale=sb_scale, mem_hd=mem_hd)
    return pl.pallas_call(
        body,
        grid=(b, t // tm),
        in_specs=[pl.BlockSpec((1, tm, d), lambda i, j: (i, j, 0)), _const_spec((1, d)),
                  _const_spec(w_a.shape), _const_spec(conv_w.shape), _const_spec((1, w)),
                  pl.BlockSpec((1, n_mem, w), lambda i, j: (i, 0, 0)),
                  pl.BlockSpec((1, n_mem, w), lambda i, j: (i, 0, 0))],
        out_specs=[blk] * 7 + [pl.BlockSpec((1, CONV_WIDTH - 1, w), lambda i, j: (i, 0, 0))],
        out_shape=[row(BF16), row(BF16), row(F32), row(F32), row(BF16), row(BF16), row(BF16),
                   jax.ShapeDtypeStruct((b, CONV_WIDTH - 1, w), F32)],
        scratch_shapes=[pltpu.VMEM((tm + V7X_SUBLANES, w), F32)],
        compiler_params=_params("arbitrary", "arbitrary"),
        name="inproj_prompt",
    )(x, g, w_a, conv_w, conv_b, mkb, mvb)


def _inproj_sample_body(x_ref, g_ref, w_ref, cw_ref, cb_ref, st0_ref, st1_ref,
                        ya_ref, q_ref, k_ref, v_ref, mq_ref, ccu_ref, *, w, sb_scale):
    h = _rms(x_ref[...], g_ref[...]).astype(BF16)

    def grp(i):
        return _dot(h, w_ref[:, i * w:(i + 1) * w])

    ccu = grp(1) * grp(2)
    cw = cw_ref[...]
    conv = cb_ref[...] + cw[2:3] * ccu
    conv = conv + cw[0:1] * st0_ref[...]
    conv = conv + cw[1:2] * st1_ref[...]
    ya_ref[...] = (grp(0) * conv).astype(BF16)
    ccu_ref[...] = ccu
    q_ref[...] = grp(3) * sb_scale
    k_ref[...] = grp(4)
    v_ref[...] = grp(5)
    mq_ref[...] = grp(6)


def _inproj_sample(x, g, w_a, conv_w, conv_b, st0, st1, *, sb_scale):
    n, d = x.shape
    w = conv_w.shape[1]
    row = lambda dt: jax.ShapeDtypeStruct((n, w), dt)
    body = functools.partial(_inproj_sample_body, w=w, sb_scale=sb_scale)
    return pl.pallas_call(
        body,
        grid=(1,),
        in_specs=[_const_spec((n, d)), _const_spec((1, d)), _const_spec(w_a.shape),
                  _const_spec(conv_w.shape), _const_spec((1, w)), _const_spec((n, w)),
                  _const_spec((n, w))],
        out_specs=[_const_spec((n, w))] * 6,
        out_shape=[row(BF16), row(F32), row(F32), row(F32), row(F32), row(F32)],
        compiler_params=_params("arbitrary"),
        name="inproj_sample",
    )(x, g, w_a, conv_w, conv_b, st0, st1)


_SIGN_BIT = -2 ** 31


def _softplus(z):
    neg_abs = lax.bitcast_convert_type(lax.bitcast_convert_type(z, jnp.int32) | _SIGN_BIT, F32)
    return jnp.maximum(z, 0.0) + jnp.log(1.0 + jnp.exp(neg_abs))


def _sb_log_weights(z, causal, u):
    sp = _softplus(z)
    if causal is not None:
        sp = jnp.where(causal, sp, 0.0)
    incl = _dot(sp.astype(BF16), u)
    return z - incl, jnp.broadcast_to(incl[:, :1], (z.shape[0], V7X_LANES))


def _cumsum_weights(n):
    row = lax.broadcasted_iota(jnp.int32, (n, n), 0)
    col = lax.broadcasted_iota(jnp.int32, (n, n), 1)
    return jnp.where(row >= col, 1.0, 0.0).astype(BF16)


def _lanes(x, n):
    return jnp.concatenate([x] * (n // V7X_LANES), axis=1)


def _sb_prompt_body(bias_ref, q_ref, k_ref, v_ref, o_ref, qs_scr, acc_scr, carry_scr,
                    lwa_scr, lwb_scr, resta_scr, restb_scr, *, tq, tk, hd):
    hp = pl.program_id(1)
    i = pl.program_id(2)
    hpb = V7X_LANES // hd
    ratio = tq // tk
    q2 = q_ref[0]
    lane_head = lax.broadcasted_iota(jnp.int32, (tq, V7X_LANES), 1) // hd
    for h in range(hpb):
        qs_scr[h * tq:(h + 1) * tq, :] = jnp.where(lane_head == h, q2, jnp.zeros_like(q2))
    u = _cumsum_weights(tk)
    acc_scr[...] = jnp.zeros_like(acc_scr)
    carry_scr[...] = jnp.zeros_like(carry_scr)

    def log_weights(j, mask):
        ks = pl.multiple_of(j * tk, tk)
        mm = _dot_nt(qs_scr[...], k_ref[0, pl.ds(ks, tk), :])
        z = jnp.concatenate([mm[h * tq:(h + 1) * tq] + bias_ref[hp * hpb + h] for h in range(hpb)], axis=0)
        return _sb_log_weights(z, mask, u)

    def values(j):
        return v_ref[0, pl.ds(pl.multiple_of(j * tk, tk), tk), :]

    def chain(j_top, masks):
        n = len(masks)
        parts = [log_weights(j_top - t, masks[t]) for t in range(n)]
        carry = carry_scr[...]
        weights = []
        for t, (lw, rest) in enumerate(parts):
            a = jnp.exp(lw - _lanes(carry, tk))
            if masks[t] is not None:
                a = jnp.where(masks[t], a, 0.0)
            weights.append(a.astype(BF16))
            carry = carry + rest
        acc_scr[...] += _dot(jnp.concatenate(weights, axis=1),
                             jnp.concatenate([values(j_top - t) for t in range(n)], axis=0))
        carry_scr[...] = carry

    row = lax.broadcasted_iota(jnp.int32, (tq, tk), 0)
    col = lax.broadcasted_iota(jnp.int32, (tq, tk), 1)
    diag_masks = [jnp.concatenate([col + d * tk < row] * hpb, axis=0) for d in range(ratio - 1, -1, -1)]
    n_full = ratio * i
    odd = lax.bitwise_and(n_full, 1)
    n_pairs = lax.shift_right_logical(n_full, 1)
    top = n_full - 1 - odd

    def front(p, lw_ref, rest_ref):
        for t in range(2):
            j = jnp.maximum(top - 2 * p - t, 0)
            lw, rest = log_weights(j, None)
            lw_ref[t] = lw
            rest_ref[t] = rest

    def back(p, lw_ref, rest_ref):
        j = top - 2 * p
        carry = carry_scr[...]
        rest1 = rest_ref[0]
        a1 = jnp.exp(lw_ref[0] - _lanes(carry, tk)).astype(BF16)
        a0 = jnp.exp(lw_ref[1] - _lanes(carry + rest1, tk)).astype(BF16)
        acc_scr[...] += _dot(jnp.concatenate([a1, a0], axis=1),
                             jnp.concatenate([values(j), values(j - 1)], axis=0))
        carry_scr[...] = carry + rest1 + rest_ref[1]

    front(0, lwa_scr, resta_scr)
    chain(ratio * i + ratio - 1, diag_masks)

    @pl.when(odd == 1)
    def _():
        chain(n_full - 1, [None])

    def loop_body(n, c):
        p = 2 * n
        front(p + 1, lwb_scr, restb_scr)
        back(p, lwa_scr, resta_scr)

        @pl.when(p + 1 < n_pairs)
        def _():
            front(p + 2, lwa_scr, resta_scr)
            back(p + 1, lwb_scr, restb_scr)

        return c

    lax.fori_loop(0, lax.shift_right_logical(n_pairs + 1, 1), loop_body, 0)

    out = acc_scr[(hpb - 1) * tq:, :]
    for h in range(hpb - 2, -1, -1):
        out = jnp.where(lane_head == h, acc_scr[h * tq:(h + 1) * tq, :], out)
    o_ref[0] = out.astype(BF16)


def _sb_prompt(q, k, v, bias, *, hd, tq, tk):
    b, t, w = q.shape
    assert tq % tk == 0 and t % tq == 0
    body = functools.partial(_sb_prompt_body, tq=tq, tk=tk, hd=hd)
    hpb = V7X_LANES // hd
    pair = (2, hpb * tq)
    grid_spec = pltpu.PrefetchScalarGridSpec(
        num_scalar_prefetch=1,
        grid=(b, w // V7X_LANES, t // tq),
        in_specs=[pl.BlockSpec((1, tq, V7X_LANES), lambda bi, hp, i, _: (bi, i, hp)),
                  pl.BlockSpec((1, t, V7X_LANES), lambda bi, hp, i, _: (bi, 0, hp)),
                  pl.BlockSpec((1, t, V7X_LANES), lambda bi, hp, i, _: (bi, 0, hp))],
        out_specs=pl.BlockSpec((1, tq, V7X_LANES), lambda bi, hp, i, _: (bi, i, hp)),
        scratch_shapes=[pltpu.VMEM((hpb * tq, V7X_LANES), BF16), pltpu.VMEM((hpb * tq, V7X_LANES), F32),
                        pltpu.VMEM((hpb * tq, V7X_LANES), F32),
                        pltpu.VMEM(pair + (tk,), F32), pltpu.VMEM(pair + (tk,), F32),
                        pltpu.VMEM(pair + (V7X_LANES,), F32), pltpu.VMEM(pair + (V7X_LANES,), F32)],
    )
    return pl.pallas_call(
        body,
        grid_spec=grid_spec,
        out_shape=jax.ShapeDtypeStruct((b, t, w), BF16),
        compiler_params=_params("arbitrary", "arbitrary", "arbitrary"),
        name="sb_prompt",
    )(bias, q, k, v)


def _sb_sample_body(pt_ref, q_ref, bias_ref, *refs, pp, hd, rows):
    k_refs = refs[:pp]
    v_refs = refs[pp:2 * pp]
    o_ref = refs[2 * pp]
    acc_scr, carry_scr = refs[2 * pp + 1:]
    g = pl.program_id(1)
    w = q_ref.shape[-1]
    page = k_refs[0].shape[2]

    @pl.when(g == 0)
    def _():
        acc_scr[...] = jnp.zeros_like(acc_scr)
        carry_scr[...] = jnp.zeros_like(carry_scr)

    row_head = lax.broadcasted_iota(jnp.int32, (rows, w), 0)
    lane_head = lax.broadcasted_iota(jnp.int32, (rows, w), 1) // hd
    on_diag = row_head == lane_head
    qrow = jnp.broadcast_to(q_ref[0], (rows, w))
    qbd = jnp.where(on_diag, qrow, 0.0).astype(BF16)
    u = _cumsum_weights(page)

    kt = jnp.concatenate([k_refs[r][0] for r in range(pp)], axis=1).astype(BF16)
    z = _dot(qbd, kt) + bias_ref[...]
    z_st = jnp.concatenate([z[:, r * page:(r + 1) * page] for r in range(pp)], axis=0)
    lw_st, rest_st = _sb_log_weights(z_st, None, u)
    after = carry_scr[...]
    afters = [None] * pp
    for r in range(pp - 1, -1, -1):
        afters[r] = after
        after = after + rest_st[r * rows:(r + 1) * rows]
    carry_scr[...] = after
    a_st = jnp.exp(lw_st - _lanes(jnp.concatenate(afters, axis=0), page)).astype(BF16)
    a = jnp.concatenate([a_st[r * rows:(r + 1) * rows] for r in range(pp)], axis=1)
    vt = jnp.concatenate([v_refs[r][0] for r in range(pp)], axis=1).astype(BF16)
    acc_scr[...] += _dot_nt(a, vt)

    @pl.when(g == pl.num_programs(1) - 1)
    def _():
        picked = jnp.where(on_diag, acc_scr[...], 0.0)
        o_ref[0] = jnp.sum(picked, axis=0, keepdims=True).astype(BF16)


def _sb_sample(q, bias_col, cache_kt, cache_vt, page_table, *, hd, pp, rows):
    n, w = q.shape
    n_pages = page_table.shape[1]
    page = cache_kt.shape[2]
    assert n_pages % pp == 0
    ng = n_pages // pp

    def page_spec(r):
        return pl.BlockSpec((1, w, page), lambda bi, g, pt: (pt[bi, (ng - 1 - g) * pp + r], 0, 0))

    body = functools.partial(_sb_sample_body, pp=pp, hd=hd, rows=rows)
    grid_spec = pltpu.PrefetchScalarGridSpec(
        num_scalar_prefetch=1,
        grid=(n, ng),
        in_specs=[pl.BlockSpec((1, 1, w), lambda bi, g, pt: (bi, 0, 0)),
                  pl.BlockSpec((rows, 1), lambda bi, g, pt: (0, 0))]
        + [page_spec(r) for r in range(pp)] * 2,
        out_specs=pl.BlockSpec((1, 1, w), lambda bi, g, pt: (bi, 0, 0)),
        scratch_shapes=[pltpu.VMEM((rows, w), F32), pltpu.VMEM((rows, V7X_LANES), F32)],
    )
    out = pl.pallas_call(
        body,
        grid_spec=grid_spec,
        out_shape=jax.ShapeDtypeStruct((n, 1, w), BF16),
        compiler_params=_params("arbitrary", "arbitrary"),
        name="sb_sample",
    )(page_table, q.reshape(n, 1, w), bias_col, *([cache_kt] * pp), *([cache_vt] * pp))
    return out.reshape(n, w)


def _mem_sample_body(q_ref, mk_ref, mv_ref, o_ref, *, heads):
    g = V7X_SUBLANES
    seqs, rows, hd = mk_ref.shape

    def over_tokens(x, op, reduce):
        y = reduce(x, axis=0)
        out = y
        for rep in range(1, g // heads):
            out = op(out, pltpu.roll(y, rep * heads, axis=0))
        return out

    def one_sequence(b, carry):
        k3 = mk_ref[b].reshape(rows // g, g, hd)
        v3 = mv_ref[b].reshape(rows // g, g, hd)
        s = jnp.sum(k3 * q_ref[b], axis=-1, keepdims=True) * (hd ** -0.5)
        s = jnp.broadcast_to(s, k3.shape)
        p = jnp.exp(s - over_tokens(s, jnp.maximum, jnp.max))
        p = p / over_tokens(p, jnp.add, jnp.sum)
        o_ref[b] = over_tokens(p * v3, jnp.add, jnp.sum)[:heads].astype(BF16)
        return carry

    lax.fori_loop(0, seqs, one_sequence, 0)


def _mem_sample(mq, mem_k, mem_v, *, seqs):
    n, heads, hd = mq.shape
    rows = mem_k.shape[1]
    assert V7X_SUBLANES % heads == 0 and n % seqs == 0
    q8 = jnp.tile(mq, (1, V7X_SUBLANES // heads, 1))
    mem = pl.BlockSpec((seqs, rows, hd), lambda i: (i, 0, 0))
    return pl.pallas_call(
        functools.partial(_mem_sample_body, heads=heads),
        grid=(n // seqs,),
        in_specs=[pl.BlockSpec((seqs, V7X_SUBLANES, hd), lambda i: (i, 0, 0)), mem, mem],
        out_specs=pl.BlockSpec((seqs, heads, hd), lambda i: (i, 0, 0)),
        out_shape=jax.ShapeDtypeStruct((n, heads, hd), BF16),
        compiler_params=_params("arbitrary"),
        name="mem_sample",
    )(q8, mem_k, mem_v)


def _mix_body(x_ref, ya_ref, yb_ref, yc_ref, gpre_ref, gpost_ref, wg_ref, wb_ref, wo_ref, o_ref):
    x = x_ref[0]
    h = _rms(x, gpre_ref[...]).astype(BF16)
    ys = (ya_ref[0], yb_ref[0], yc_ref[0])
    merged = None
    for n in range(N_BRANCH):
        gate = 1.0 / (1.0 + jnp.exp(-_dot(h, wg_ref[n])))
        term = gate * _dot(ys[n], wb_ref[n])
        merged = term if merged is None else merged + term
    mixed = _dot(merged.astype(BF16), wo_ref[...])
    o_ref[0] = x + _rms(mixed, gpost_ref[...])


def _mix(x, ya, yb, yc, g_pre, g_post, w_gate, w_branch, w_o, *, tm):
    b, t, d = x.shape
    w = ya.shape[-1]
    xblk = pl.BlockSpec((1, tm, d), lambda i, j: (i, j, 0))
    yblk = pl.BlockSpec((1, tm, w), lambda i, j: (i, j, 0))
    return pl.pallas_call(
        _mix_body,
        grid=(b, t // tm),
        in_specs=[xblk, yblk, yblk, yblk, _const_spec((1, d)), _const_spec((1, d)),
                  _const_spec(w_gate.shape), _const_spec(w_branch.shape), _const_spec(w_o.shape)],
        out_specs=xblk,
        out_shape=jax.ShapeDtypeStruct((b, t, d), F32),
        compiler_params=_params("arbitrary", "arbitrary"),
        name="mix",
    )(x, ya, yb, yc, g_pre, g_post, w_gate, w_branch, w_o)


def _gelu_tanh(x):
    return x * (0.5 * (1.0 + jnp.tanh((2.0 / jnp.pi) ** 0.5 * (x + 0.044715 * (x * x * x)))))


def _ffn_prompt_body(x_ref, gpre_ref, gpost_ref, wup_ref, fcw_ref, fcb_ref, wdn_ref,
                     o_ref, st_ref, hn_scr, act_scr, carry_scr, s_scr, *, tm, nc):
    ti = pl.program_id(1)
    width = wup_ref.shape[2]

    @pl.when(ti == 0)
    def _():
        carry_scr[...] = jnp.zeros_like(carry_scr)

    x = x_ref[0]
    hn_scr[...] = _rms(x, gpre_ref[...]).astype(BF16)

    def conv_chunk(idx, slot):
        u = _dot(hn_scr[...], wup_ref[idx])
        s_scr[slot, 0:V7X_SUBLANES, :] = carry_scr[idx]
        s_scr[slot, V7X_SUBLANES:, :] = u
        last = u[tm - V7X_SUBLANES:, :]
        carry_scr[idx] = last
        st_ref[0, idx] = last
        r1 = s_scr[slot, pl.ds(V7X_SUBLANES - 1, tm), :]
        r2 = s_scr[slot, pl.ds(V7X_SUBLANES - 2, tm), :]
        cw = fcw_ref[idx]
        y = fcb_ref[idx] + cw[2:3] * u
        y = y + cw[0:1] * r2
        return y + cw[1:2] * r1

    for c in range(nc):
        a = conv_chunk(c, 2 * (c % 2))
        bv = conv_chunk(c + nc, 2 * (c % 2) + 1)
        act_scr[:, c * width:(c + 1) * width] = (_gelu_tanh(a) * bv).astype(BF16)
    f = _dot(act_scr[...], wdn_ref[...])
    o_ref[0] = x + _rms(f, gpost_ref[...])


def _ffn_prompt(x, g_pre, g_post, w_up_c, fcw_c, fcb_c, w_dn, *, tm):
    b, t, d = x.shape
    nc2, _, cw = w_up_c.shape
    nc = nc2 // 2
    xblk = pl.BlockSpec((1, tm, d), lambda i, j: (i, j, 0))
    body = functools.partial(_ffn_prompt_body, tm=tm, nc=nc)
    return pl.pallas_call(
        body,
        grid=(b, t // tm),
        in_specs=[xblk, _const_spec((1, d)), _const_spec((1, d)), _const_spec(w_up_c.shape),
                  _const_spec(fcw_c.shape), _const_spec(fcb_c.shape), _const_spec(w_dn.shape)],
        out_specs=[xblk, pl.BlockSpec((1, nc2, V7X_SUBLANES, cw), lambda i, j: (i, 0, 0, 0))],
        out_shape=[jax.ShapeDtypeStruct((b, t, d), F32),
                   jax.ShapeDtypeStruct((b, nc2, V7X_SUBLANES, cw), F32)],
        scratch_shapes=[pltpu.VMEM((tm, d), BF16), pltpu.VMEM((tm, nc * cw), BF16),
                        pltpu.VMEM((nc2, V7X_SUBLANES, cw), F32),
                        pltpu.VMEM((4, tm + V7X_SUBLANES, cw), F32)],
        compiler_params=_params("arbitrary", "arbitrary"),
        name="ffn_prompt",
    )(x, g_pre, g_post, w_up_c, fcw_c, fcb_c, w_dn)


def _ffn_sample_body(x_ref, gpre_ref, gpost_ref, wup_ref, fcw_ref, fcb_ref, wdn_ref, st0_ref, st1_ref,
                     o_ref, up_ref, hn_scr, f_scr, *, nc):
    x = x_ref[...]
    hn_scr[...] = _rms(x, gpre_ref[...]).astype(BF16)
    f_scr[...] = jnp.zeros_like(f_scr)

    def conv_chunk(idx):
        u = _dot(hn_scr[...], wup_ref[idx])
        up_ref[idx] = u
        cw = fcw_ref[idx]
        y = fcb_ref[idx] + cw[2:3] * u
        y = y + cw[0:1] * st0_ref[idx]
        return y + cw[1:2] * st1_ref[idx]

    def chunk(c, carry):
        act = (_gelu_tanh(conv_chunk(c)) * conv_chunk(c + nc)).astype(BF16)
        f_scr[...] += _dot(act, wdn_ref[c])
        return carry

    lax.fori_loop(0, nc, chunk, 0)
    o_ref[...] = x + _rms(f_scr[...], gpost_ref[...])


def _ffn_sample(x, g_pre, g_post, w_up_c, fcw_c, fcb_c, w_dn_c, st0_c, st1_c):
    n, d = x.shape
    nc2, _, cw = w_up_c.shape
    body = functools.partial(_ffn_sample_body, nc=nc2 // 2)
    return pl.pallas_call(
        body,
        grid=(1,),
        in_specs=[_const_spec((n, d)), _const_spec((1, d)), _const_spec((1, d)), _const_spec(w_up_c.shape),
                  _const_spec(fcw_c.shape), _const_spec(fcb_c.shape), _const_spec(w_dn_c.shape),
                  _const_spec(st0_c.shape), _const_spec(st1_c.shape)],
        out_specs=[_const_spec((n, d)), _const_spec((nc2, n, cw))],
        out_shape=[jax.ShapeDtypeStruct((n, d), F32), jax.ShapeDtypeStruct((nc2, n, cw), F32)],
        scratch_shapes=[pltpu.VMEM((n, d), BF16), pltpu.VMEM((n, d), F32)],
        compiler_params=_params("arbitrary"),
        name="ffn_sample",
    )(x, g_pre, g_post, w_up_c, fcw_c, fcb_c, w_dn_c, st0_c, st1_c)


FFN_CHUNK = 2 * V7X_LANES
SAMPLE_PAGES_PER_STEP = 32
SAMPLE_MEM_SEQS_PER_STEP = 8
SAMPLE_HEAD_ROWS = 2 * V7X_SUBLANES


def _chunk_cols(a, cw):
    n = a.shape[-1]
    a = a.reshape(a.shape[:-1] + (n // cw, cw))
    return jnp.moveaxis(a, -2, 0)


def _unchunk_cols(a):
    nc, r, cw = a.shape
    return jnp.moveaxis(a, 0, 1).reshape(r, nc * cw)


def kernel(x_prompt, x_sample, cache_sb_k, cache_sb_v, cache_mem_k, cache_mem_v, state_conv, state_ffn_conv,
           page_table, mem_prompt, g_mix_pre, g_mix_post, g_ffn_pre, g_ffn_post, g_mem, w_in, conv_w, conv_b,
           sb_bias, w_mem_kv, w_branch, w_o, w_up, ffn_conv_w, ffn_conv_b, w_down):
    depth = w_in.shape[0]
    bp, seq, d = x_prompt.shape
    bs, dec_seq, _ = x_sample.shape
    assert dec_seq == 1, "the sample group advances one token per step"
    _, n_pool, page, sb_heads, sb_hd = cache_sb_k.shape
    _, _, n_mem, mem_heads, mem_hd = cache_mem_k.shape
    w = sb_heads * sb_hd
    d_ff = w_down.shape[1]
    assert mem_hd % V7X_LANES == 0 and V7X_LANES % sb_hd == 0 and d_ff % FFN_CHUNK == 0
    sb_scale = sb_hd ** -0.5
    assert math.frexp(sb_scale)[0] == 0.5, "the logit scale must be a power of two to fold into q exactly"

    tm_in = _pick_tile(seq, 512)
    tm_mix = _pick_tile(seq, 512)
    tm_ffn = _pick_tile(seq, 512)
    tq = _pick_tile(seq, 2 * V7X_LANES)
    tk = _pick_tile(tq, 2 * V7X_LANES)

    xp = x_prompt
    xs = x_sample.reshape(bs, d)
    conv_p, k_p, v_p, mk_p, mv_p, ffn_p = [], [], [], [], [], []
    conv_s, k_s, v_s, ffn_s = [], [], [], []
    for l in range(depth):
        row = lambda a: a[l].reshape(1, -1)
        w_a = w_in[l][:, :7 * w].astype(BF16)
        w_gate = jnp.moveaxis(w_in[l][:, 7 * w:].reshape(d, N_BRANCH, d), 1, 0).astype(BF16)
        w_br = w_branch[l].astype(BF16)
        w_out = w_o[l].astype(BF16)
        w_up_c = _chunk_cols(w_up[l], FFN_CHUNK).astype(BF16)
        w_dn_c = w_down[l].reshape(d_ff // FFN_CHUNK, FFN_CHUNK, d).astype(BF16)
        fcw_c = _chunk_cols(ffn_conv_w[l], FFN_CHUNK)
        fcb_c = _chunk_cols(ffn_conv_b[l].reshape(1, -1), FFN_CHUNK)
        bias = sb_bias[l]

        mk, mv, mkb, mvb = _memkv(mem_prompt, row(g_mem), w_mem_kv[l].astype(BF16))
        ya, q, k, v, kb, vb, yc, cs = _inproj_prompt(
            xp, row(g_mix_pre), w_a, conv_w[l], row(conv_b), mkb, mvb,
            sb_scale=sb_scale, mem_hd=mem_hd, tm=tm_in)
        yb = _sb_prompt(q, kb, vb, bias, hd=sb_hd, tq=tq, tk=tk)
        x1 = _mix(xp, ya, yb, yc, row(g_mix_pre), row(g_mix_post), w_gate, w_br, w_out, tm=tm_mix)
        xp, st = _ffn_prompt(x1, row(g_ffn_pre), row(g_ffn_post), w_up_c, fcw_c, fcb_c,
                             w_dn_c.reshape(d_ff, d), tm=tm_ffn)
        fs = jnp.moveaxis(st[:, :, V7X_SUBLANES - (CONV_WIDTH - 1):, :], 1, 2).reshape(bp, CONV_WIDTH - 1, -1)
        conv_p.append(cs)
        k_p.append(k.reshape(bp, seq, sb_heads, sb_hd))
        v_p.append(v.reshape(bp, seq, sb_heads, sb_hd))
        mk_p.append(mk.reshape(bp, n_mem, mem_heads, mem_hd))
        mv_p.append(mv.reshape(bp, n_mem, mem_heads, mem_hd))
        ffn_p.append(fs)

        st_c = state_conv[l]
        ya, q, k, v, mq, ccu = _inproj_sample(xs, row(g_mix_pre), w_a, conv_w[l], row(conv_b),
                                              st_c[:, 0], st_c[:, 1], sb_scale=sb_scale)
        bias_col = jnp.zeros((SAMPLE_HEAD_ROWS, 1), F32).at[:sb_heads, 0].set(bias)
        feature_major = lambda c: jnp.transpose(c[l], (0, 2, 3, 1)).reshape(n_pool, w, page)
        yb = _sb_sample(q, bias_col, feature_major(cache_sb_k), feature_major(cache_sb_v),
                        page_table, hd=sb_hd, pp=SAMPLE_PAGES_PER_STEP, rows=SAMPLE_HEAD_ROWS)
        yc = _mem_sample(mq.reshape(bs, mem_heads, mem_hd),
                         cache_mem_k[l].reshape(bs, n_mem * mem_heads, mem_hd),
                         cache_mem_v[l].reshape(bs, n_mem * mem_heads, mem_hd),
                         seqs=_pick_tile(bs, SAMPLE_MEM_SEQS_PER_STEP)).reshape(bs, w)
        x1 = _mix(xs[None], ya[None], yb[None], yc[None], row(g_mix_pre), row(g_mix_post),
                  w_gate, w_br, w_out, tm=bs)[0]
        st_f = state_ffn_conv[l]
        xs, up_c = _ffn_sample(x1, row(g_ffn_pre), row(g_ffn_post), w_up_c, fcw_c, fcb_c, w_dn_c,
                               _chunk_cols(st_f[:, 0], FFN_CHUNK), _chunk_cols(st_f[:, 1], FFN_CHUNK))
        conv_s.append(jnp.stack([st_c[:, 1], ccu], axis=1))
        k_s.append(k.reshape(bs, 1, sb_heads, sb_hd))
        v_s.append(v.reshape(bs, 1, sb_heads, sb_hd))
        ffn_s.append(jnp.stack([st_f[:, 1], _unchunk_cols(up_c)], axis=1))

    return (xp, xs.reshape(bs, 1, d), jnp.stack(conv_p), jnp.stack(k_p), jnp.stack(v_p), jnp.stack(mk_p),
            jnp.stack(mv_p), jnp.stack(ffn_p), jnp.stack(conv_s), jnp.stack(k_s), jnp.stack(v_s),
            jnp.stack(ffn_s))
```

```python
import functools
import math

import jax
import jax.numpy as jnp
from jax import lax
from jax.experimental import pallas as pl
from jax.experimental.pallas import tpu as pltpu

EPS = 1e-6
CONV_WIDTH = 3
N_BRANCH = 3
F32 = jnp.float32
BF16 = jnp.bfloat16

V7X_LANES = 128
V7X_SUBLANES = 8
V7X_VMEM_BYTES = 64 * 1024 * 1024
VMEM_LIMIT_BYTES = (V7X_VMEM_BYTES * 7) // 8


def _params(*semantics):
    return pltpu.CompilerParams(dimension_semantics=semantics, vmem_limit_bytes=VMEM_LIMIT_BYTES)


def _rms(x, g):
    ms = jnp.mean(x * x, axis=-1, keepdims=True)
    return (x * lax.rsqrt(ms + EPS)) * g


def _dot(a, b):
    return jnp.dot(a, b, preferred_element_type=F32)


def _dot_nt(a, b):
    return lax.dot_general(a, b, (((1,), (1,)), ((), ())), preferred_element_type=F32)


def _const_spec(shape):
    nd = len(shape)
    return pl.BlockSpec(shape, lambda *_: (0,) * nd, pipeline_mode=pl.Buffered(1))


def _pick_tile(n, want):
    t = min(n, want)
    assert n % t == 0, (n, t)
    return t


def _memkv_body(mem_ref, g_ref, w_ref, mk_ref, mv_ref, mkb_ref, mvb_ref):
    h = _rms(mem_ref[0], g_ref[...]).astype(BF16)
    kv = _dot(h, w_ref[...])
    w = kv.shape[1] // 2
    mk_ref[0] = kv[:, :w]
    mv_ref[0] = kv[:, w:]
    mkb_ref[0] = kv[:, :w].astype(BF16)
    mvb_ref[0] = kv[:, w:].astype(BF16)


def _memkv(mem, g_mem, w_mem_kv_bf):
    b, n_mem, d = mem.shape
    w = w_mem_kv_bf.shape[1] // 2
    blk = pl.BlockSpec((1, n_mem, w), lambda i: (i, 0, 0))
    return pl.pallas_call(
        _memkv_body,
        grid=(b,),
        in_specs=[pl.BlockSpec((1, n_mem, d), lambda i: (i, 0, 0)), _const_spec((1, d)),
                  _const_spec((d, 2 * w))],
        out_specs=[blk, blk, blk, blk],
        out_shape=[jax.ShapeDtypeStruct((b, n_mem, w), F32)] * 2
        + [jax.ShapeDtypeStruct((b, n_mem, w), BF16)] * 2,
        compiler_params=_params("arbitrary"),
        name="memkv",
    )(mem, g_mem, w_mem_kv_bf)


def _softmax_rows(s):
    m = jnp.max(s, axis=-1, keepdims=True)
    p = jnp.exp(s - m)
    return p / jnp.sum(p, axis=-1, keepdims=True)


def _inproj_prompt_body(x_ref, g_ref, w_ref, cw_ref, cb_ref, mk_ref, mv_ref,
                        ya_ref, q_ref, k_ref, v_ref, kb_ref, vb_ref, yc_ref, cs_ref,
                        s_scr, *, tm, w, sb_scale, mem_hd):
    ti = pl.program_id(1)
    h = _rms(x_ref[0], g_ref[...]).astype(BF16)

    def grp(i):
        return _dot(h, w_ref[:, i * w:(i + 1) * w])

    ccu = grp(1) * grp(2)
    @pl.when(ti == 0)
    def _():
        s_scr[tm:tm + V7X_SUBLANES, :] = jnp.zeros((V7X_SUBLANES, w), F32)

    s_scr[0:V7X_SUBLANES, :] = s_scr[tm:tm + V7X_SUBLANES, :]
    s_scr[V7X_SUBLANES:, :] = ccu
    r1 = s_scr[pl.ds(V7X_SUBLANES - 1, tm), :]
    r2 = s_scr[pl.ds(V7X_SUBLANES - 2, tm), :]
    cw = cw_ref[...]
    conv = cb_ref[...] + cw[2:3] * ccu
    conv = conv + cw[0:1] * r2
    conv = conv + cw[1:2] * r1
    ya_ref[0] = (grp(0) * conv).astype(BF16)
    cs_ref[0] = s_scr[pl.ds(tm + V7X_SUBLANES - (CONV_WIDTH - 1), CONV_WIDTH - 1), :]

    q_ref[0] = (grp(3) * sb_scale).astype(BF16)
    k = grp(4)
    k_ref[0] = k
    kb_ref[0] = k.astype(BF16)
    v = grp(5)
    v_ref[0] = v
    vb_ref[0] = v.astype(BF16)

    mq = grp(6)
    outs = []
    for hd in range(w // mem_hd):
        sl = slice(hd * mem_hd, (hd + 1) * mem_hd)
        s = _dot_nt(mq[:, sl].astype(BF16), mk_ref[0, :, sl]) * (mem_hd ** -0.5)
        p = _softmax_rows(s)
        outs.append(_dot(p.astype(BF16), mv_ref[0, :, sl]))
    yc_ref[0] = jnp.concatenate(outs, axis=1).astype(BF16)


def _inproj_prompt(x, g, w_a, conv_w, conv_b, mkb, mvb, *, sb_scale, mem_hd, tm):
    b, t, d = x.shape
    w = conv_w.shape[1]
    n_mem = mkb.shape[1]
    row = lambda dt: jax.ShapeDtypeStruct((b, t, w), dt)
    blk = pl.BlockSpec((1, tm, w), lambda i, j: (i, j, 0))
    body = functools.partial(_inproj_prompt_body, tm=tm, w=w, sb_scale=sb_scale, mem_hd=mem_hd)
    return pl.pallas_call(
        body,
        grid=(b, t // tm),
        in_specs=[pl.BlockSpec((1, tm, d), lambda i, j: (i, j, 0)), _const_spec((1, d)),
                  _const_spec(w_a.shape), _const_spec(conv_w.shape), _const_spec((1, w)),
                  pl.BlockSpec((1, n_mem, w), lambda i, j: (i, 0, 0)),
                  pl.BlockSpec((1, n_mem, w), lambda i, j: (i, 0, 0))],
        out_specs=[blk] * 7 + [pl.BlockSpec((1, CONV_WIDTH - 1, w), lambda i, j: (i, 0, 0))],
        out_shape=[row(BF16), row(BF16), row(F32), row(F32), row(BF16), row(BF16), row(BF16),
                   jax.ShapeDtypeStruct((b, CONV_WIDTH - 1, w), F32)],
        scratch_shapes=[pltpu.VMEM((tm + V7X_SUBLANES, w), F32)],
        compiler_params=_params("arbitrary", "arbitrary"),
        name="inproj_prompt",
    )(x, g, w_a, conv_w, conv_b, mkb, mvb)


def _inproj_sample_body(x_ref, g_ref, w_ref, cw_ref, cb_ref, st0_ref, st1_ref,
                        ya_ref, q_ref, k_ref, v_ref, mq_ref, ccu_ref, *, w, sb_scale):
    h = _rms(x_ref[...], g_ref[...]).astype(BF16)

    def grp(i):
        return _dot(h, w_ref[:, i * w:(i + 1) * w])

    ccu = grp(1) * grp(2)
    cw = cw_ref[...]
    conv = cb_ref[...] + cw[2:3] * ccu
    conv = conv + cw[0:1] * st0_ref[...]
    conv = conv + cw[1:2] * st1_ref[...]
    ya_ref[...] = (grp(0) * conv).astype(BF16)
    ccu_ref[...] = ccu
    q_ref[...] = grp(3) * sb_scale
    k_ref[...] = grp(4)
    v_ref[...] = grp(5)
    mq_ref[...] = grp(6)


def _inproj_sample(x, g, w_a, conv_w, conv_b, st0, st1, *, sb_scale):
    n, d = x.shape
    w = conv_w.shape[1]
    row = lambda dt: jax.ShapeDtypeStruct((n, w), dt)
    body = functools.partial(_inproj_sample_body, w=w, sb_scale=sb_scale)
    return pl.pallas_call(
        body,
        grid=(1,),
        in_specs=[_const_spec((n, d)), _const_spec((1, d)), _const_spec(w_a.shape),
                  _const_spec(conv_w.shape), _const_spec((1, w)), _const_spec((n, w)),
                  _const_spec((n, w))],
        out_specs=[_const_spec((n, w))] * 6,
        out_shape=[row(BF16), row(F32), row(F32), row(F32), row(F32), row(F32)],
        compiler_params=_params("arbitrary"),
        name="inproj_sample",
    )(x, g, w_a, conv_w, conv_b, st0, st1)


_SIGN_BIT = -2 ** 31


def _softplus(z):
    neg_abs = lax.bitcast_convert_type(lax.bitcast_convert_type(z, jnp.int32) | _SIGN_BIT, F32)
    return jnp.maximum(z, 0.0) + jnp.log(1.0 + jnp.exp(neg_abs))


def _sb_log_weights(z, causal, u):
    sp = _softplus(z)
    if causal is not None:
        sp = jnp.where(causal, sp, 0.0)
    after = _dot(sp.astype(BF16), u)
    total = after[:, :1] + sp[:, :1]
    return (z - sp) - after, jnp.broadcast_to(total, (z.shape[0], V7X_LANES))


def _cumsum_weights(n):
    row = lax.broadcasted_iota(jnp.int32, (n, n), 0)
    col = lax.broadcasted_iota(jnp.int32, (n, n), 1)
    return jnp.where(row > col, 1.0, 0.0).astype(BF16)


def _lanes(x, n):
    return jnp.concatenate([x] * (n // V7X_LANES), axis=1)


def _sb_prompt_body(bias_ref, q_ref, k_ref, v_ref, o_ref, qs_scr, acc_scr, carry_scr,
                    lwa_scr, lwb_scr, resta_scr, restb_scr, *, tq, tk, hd):
    hp = pl.program_id(1)
    i = pl.program_id(2)
    hpb = V7X_LANES // hd
    ratio = tq // tk
    q2 = q_ref[0]
    lane_head = lax.broadcasted_iota(jnp.int32, (tq, V7X_LANES), 1) // hd
    for h in range(hpb):
        qs_scr[h * tq:(h + 1) * tq, :] = jnp.where(lane_head == h, q2, jnp.zeros_like(q2))
    u = _cumsum_weights(tk)
    acc_scr[...] = jnp.zeros_like(acc_scr)
    carry_scr[...] = jnp.zeros_like(carry_scr)

    def log_weights(j, mask):
        ks = pl.multiple_of(j * tk, tk)
        mm = _dot_nt(qs_scr[...], k_ref[0, pl.ds(ks, tk), :])
        z = jnp.concatenate([mm[h * tq:(h + 1) * tq] + bias_ref[hp * hpb + h] for h in range(hpb)], axis=0)
        return _sb_log_weights(z, mask, u)

    def values(j):
        return v_ref[0, pl.ds(pl.multiple_of(j * tk, tk), tk), :]

    def chain(j_top, masks):
        n = len(masks)
        parts = [log_weights(j_top - t, masks[t]) for t in range(n)]
        carry = carry_scr[...]
        weights = []
        for t, (lw, rest) in enumerate(parts):
            a = jnp.exp(lw - _lanes(carry, tk))
            if masks[t] is not None:
                a = jnp.where(masks[t], a, 0.0)
            weights.append(a.astype(BF16))
            carry = carry + rest
        acc_scr[...] += _dot(jnp.concatenate(weights, axis=1),
                             jnp.concatenate([values(j_top - t) for t in range(n)], axis=0))
        carry_scr[...] = carry

    row = lax.broadcasted_iota(jnp.int32, (tq, tk), 0)
    col = lax.broadcasted_iota(jnp.int32, (tq, tk), 1)
    diag_masks = [jnp.concatenate([col + d * tk < row] * hpb, axis=0) for d in range(ratio - 1, -1, -1)]
    n_full = ratio * i
    odd = lax.bitwise_and(n_full, 1)
    n_pairs = lax.shift_right_logical(n_full, 1)
    top = n_full - 1 - odd

    def front(p, lw_ref, rest_ref):
        for t in range(2):
            j = jnp.maximum(top - 2 * p - t, 0)
            lw, rest = log_weights(j, None)
            lw_ref[t] = lw
            rest_ref[t] = rest

    def back(p, lw_ref, rest_ref):
        j = top - 2 * p
        carry = carry_scr[...]
        rest1 = rest_ref[0]
        a1 = jnp.exp(lw_ref[0] - _lanes(carry, tk)).astype(BF16)
        a0 = jnp.exp(lw_ref[1] - _lanes(carry + rest1, tk)).astype(BF16)
        acc_scr[...] += _dot(jnp.concatenate([a1, a0], axis=1),
                             jnp.concatenate([values(j), values(j - 1)], axis=0))
        carry_scr[...] = carry + rest1 + rest_ref[1]

    front(0, lwa_scr, resta_scr)
    chain(ratio * i + ratio - 1, diag_masks)

    @pl.when(odd == 1)
    def _():
        chain(n_full - 1, [None])

    def loop_body(n, c):
        p = 2 * n
        front(p + 1, lwb_scr, restb_scr)
        back(p, lwa_scr, resta_scr)

        @pl.when(p + 1 < n_pairs)
        def _():
            front(p + 2, lwa_scr, resta_scr)
            back(p + 1, lwb_scr, restb_scr)

        return c

    lax.fori_loop(0, lax.shift_right_logical(n_pairs + 1, 1), loop_body, 0)

    out = acc_scr[(hpb - 1) * tq:, :]
    for h in range(hpb - 2, -1, -1):
        out = jnp.where(lane_head == h, acc_scr[h * tq:(h + 1) * tq, :], out)
    o_ref[0] = out.astype(BF16)


def _sb_prompt(q, k, v, bias, *, hd, tq, tk):
    b, t, w = q.shape
    assert tq % tk == 0 and t % tq == 0
    body = functools.partial(_sb_prompt_body, tq=tq, tk=tk, hd=hd)
    hpb = V7X_LANES // hd
    pair = (2, hpb * tq)
    grid_spec = pltpu.PrefetchScalarGridSpec(
        num_scalar_prefetch=1,
        grid=(b, w // V7X_LANES, t // tq),
        in_specs=[pl.BlockSpec((1, tq, V7X_LANES), lambda bi, hp, i, _: (bi, i, hp)),
                  pl.BlockSpec((1, t, V7X_LANES), lambda bi, hp, i, _: (bi, 0, hp)),
                  pl.BlockSpec((1, t, V7X_LANES), lambda bi, hp, i, _: (bi, 0, hp))],
        out_specs=pl.BlockSpec((1, tq, V7X_LANES), lambda bi, hp, i, _: (bi, i, hp)),
        scratch_shapes=[pltpu.VMEM((hpb * tq, V7X_LANES), BF16), pltpu.VMEM((hpb * tq, V7X_LANES), F32),
                        pltpu.VMEM((hpb * tq, V7X_LANES), F32),
                        pltpu.VMEM(pair + (tk,), F32), pltpu.VMEM(pair + (tk,), F32),
                        pltpu.VMEM(pair + (V7X_LANES,), F32), pltpu.VMEM(pair + (V7X_LANES,), F32)],
    )
    return pl.pallas_call(
        body,
        grid_spec=grid_spec,
        out_shape=jax.ShapeDtypeStruct((b, t, w), BF16),
        compiler_params=_params("arbitrary", "arbitrary", "arbitrary"),
        name="sb_prompt",
    )(bias, q, k, v)


def _sb_sample_body(pt_ref, q_ref, bias_ref, *refs, pp, hd, rows):
    k_refs = refs[:pp]
    v_refs = refs[pp:2 * pp]
    o_ref = refs[2 * pp]
    acc_scr, carry_scr = refs[2 * pp + 1:]
    g = pl.program_id(1)
    w = q_ref.shape[-1]
    page = k_refs[0].shape[2]

    @pl.when(g == 0)
    def _():
        acc_scr[...] = jnp.zeros_like(acc_scr)
        carry_scr[...] = jnp.zeros_like(carry_scr)

    row_head = lax.broadcasted_iota(jnp.int32, (rows, w), 0)
    lane_head = lax.broadcasted_iota(jnp.int32, (rows, w), 1) // hd
    on_diag = row_head == lane_head
    qrow = jnp.broadcast_to(q_ref[0], (rows, w))
    qbd = jnp.where(on_diag, qrow, 0.0).astype(BF16)
    u = _cumsum_weights(page)

    kt = jnp.concatenate([k_refs[r][0] for r in range(pp)], axis=1).astype(BF16)
    z = _dot(qbd, kt) + bias_ref[...]
    z_st = jnp.concatenate([z[:, r * page:(r + 1) * page] for r in range(pp)], axis=0)
    lw_st, rest_st = _sb_log_weights(z_st, None, u)
    after = carry_scr[...]
    afters = [None] * pp
    for r in range(pp - 1, -1, -1):
        afters[r] = after
        after = after + rest_st[r * rows:(r + 1) * rows]
    carry_scr[...] = after
    a_st = jnp.exp(lw_st - _lanes(jnp.concatenate(afters, axis=0), page)).astype(BF16)
    a = jnp.concatenate([a_st[r * rows:(r + 1) * rows] for r in range(pp)], axis=1)
    vt = jnp.concatenate([v_refs[r][0] for r in range(pp)], axis=1).astype(BF16)
    acc_scr[...] += _dot_nt(a, vt)

    @pl.when(g == pl.num_programs(1) - 1)
    def _():
        picked = jnp.where(on_diag, acc_scr[...], 0.0)
        o_ref[0] = jnp.sum(picked, axis=0, keepdims=True).astype(BF16)


def _sb_sample(q, bias_col, cache_kt, cache_vt, page_table, *, hd, pp, rows):
    n, w = q.shape
    n_pages = page_table.shape[1]
    page = cache_kt.shape[2]
    assert n_pages % pp == 0
    ng = n_pages // pp

    def page_spec(r):
        return pl.BlockSpec((1, w, page), lambda bi, g, pt: (pt[bi, (ng - 1 - g) * pp + r], 0, 0))

    body = functools.partial(_sb_sample_body, pp=pp, hd=hd, rows=rows)
    grid_spec = pltpu.PrefetchScalarGridSpec(
        num_scalar_prefetch=1,
        grid=(n, ng),
        in_specs=[pl.BlockSpec((1, 1, w), lambda bi, g, pt: (bi, 0, 0)),
                  pl.BlockSpec((rows, 1), lambda bi, g, pt: (0, 0))]
        + [page_spec(r) for r in range(pp)] * 2,
        out_specs=pl.BlockSpec((1, 1, w), lambda bi, g, pt: (bi, 0, 0)),
        scratch_shapes=[pltpu.VMEM((rows, w), F32), pltpu.VMEM((rows, V7X_LANES), F32)],
    )
    out = pl.pallas_call(
        body,
        grid_spec=grid_spec,
        out_shape=jax.ShapeDtypeStruct((n, 1, w), BF16),
        compiler_params=_params("arbitrary", "arbitrary"),
        name="sb_sample",
    )(page_table, q.reshape(n, 1, w), bias_col, *([cache_kt] * pp), *([cache_vt] * pp))
    return out.reshape(n, w)


def _mem_sample_body(q_ref, mk_ref, mv_ref, o_ref, *, heads):
    g = V7X_SUBLANES
    seqs, rows, hd = mk_ref.shape

    def over_tokens(x, op, reduce):
        y = reduce(x, axis=0)
        out = y
        for rep in range(1, g // heads):
            out = op(out, pltpu.roll(y, rep * heads, axis=0))
        return out

    def one_sequence(b, carry):
        k3 = mk_ref[b].reshape(rows // g, g, hd)
        v3 = mv_ref[b].reshape(rows // g, g, hd)
        s = jnp.sum(k3 * q_ref[b], axis=-1, keepdims=True) * (hd ** -0.5)
        s = jnp.broadcast_to(s, k3.shape)
        p = jnp.exp(s - over_tokens(s, jnp.maximum, jnp.max))
        p = p / over_tokens(p, jnp.add, jnp.sum)
        o_ref[b] = over_tokens(p * v3, jnp.add, jnp.sum)[:heads].astype(BF16)
        return carry

    lax.fori_loop(0, seqs, one_sequence, 0)


def _mem_sample(mq, mem_k, mem_v, *, seqs):
    n, heads, hd = mq.shape
    rows = mem_k.shape[1]
    assert V7X_SUBLANES % heads == 0 and n % seqs == 0
    q8 = jnp.tile(mq, (1, V7X_SUBLANES // heads, 1))
    mem = pl.BlockSpec((seqs, rows, hd), lambda i: (i, 0, 0))
    return pl.pallas_call(
        functools.partial(_mem_sample_body, heads=heads),
        grid=(n // seqs,),
        in_specs=[pl.BlockSpec((seqs, V7X_SUBLANES, hd), lambda i: (i, 0, 0)), mem, mem],
        out_specs=pl.BlockSpec((seqs, heads, hd), lambda i: (i, 0, 0)),
        out_shape=jax.ShapeDtypeStruct((n, heads, hd), BF16),
        compiler_params=_params("arbitrary"),
        name="mem_sample",
    )(q8, mem_k, mem_v)


def _mix_body(x_ref, ya_ref, yb_ref, yc_ref, gpre_ref, gpost_ref, wg_ref, wb_ref, wo_ref, o_ref):
    x = x_ref[0]
    h = _rms(x, gpre_ref[...]).astype(BF16)
    ys = (ya_ref[0], yb_ref[0], yc_ref[0])
    merged = None
    for n in range(N_BRANCH):
        gate = 1.0 / (1.0 + jnp.exp(-_dot(h, wg_ref[n])))
        term = gate * _dot(ys[n], wb_ref[n])
        merged = term if merged is None else merged + term
    mixed = _dot(merged.astype(BF16), wo_ref[...])
    o_ref[0] = x + _rms(mixed, gpost_ref[...])


def _mix(x, ya, yb, yc, g_pre, g_post, w_gate, w_branch, w_o, *, tm):
    b, t, d = x.shape
    w = ya.shape[-1]
    xblk = pl.BlockSpec((1, tm, d), lambda i, j: (i, j, 0))
    yblk = pl.BlockSpec((1, tm, w), lambda i, j: (i, j, 0))
    return pl.pallas_call(
        _mix_body,
        grid=(b, t // tm),
        in_specs=[xblk, yblk, yblk, yblk, _const_spec((1, d)), _const_spec((1, d)),
                  _const_spec(w_gate.shape), _const_spec(w_branch.shape), _const_spec(w_o.shape)],
        out_specs=xblk,
        out_shape=jax.ShapeDtypeStruct((b, t, d), F32),
        compiler_params=_params("arbitrary", "arbitrary"),
        name="mix",
    )(x, ya, yb, yc, g_pre, g_post, w_gate, w_branch, w_o)


def _gelu_tanh(x):
    return x * (0.5 * (1.0 + jnp.tanh((2.0 / jnp.pi) ** 0.5 * (x + 0.044715 * (x * x * x)))))


def _ffn_prompt_body(x_ref, gpre_ref, gpost_ref, wup_ref, fcw_ref, fcb_ref, wdn_ref,
                     o_ref, st_ref, hn_scr, act_scr, carry_scr, s_scr, *, tm, nc):
    ti = pl.program_id(1)
    width = wup_ref.shape[2]

    @pl.when(ti == 0)
    def _():
        carry_scr[...] = jnp.zeros_like(carry_scr)

    x = x_ref[0]
    hn_scr[...] = _rms(x, gpre_ref[...]).astype(BF16)

    def conv_chunk(idx, slot):
        u = _dot(hn_scr[...], wup_ref[idx])
        s_scr[slot, 0:V7X_SUBLANES, :] = carry_scr[idx]
        s_scr[slot, V7X_SUBLANES:, :] = u
        last = u[tm - V7X_SUBLANES:, :]
        carry_scr[idx] = last
        st_ref[0, idx] = last
        r1 = s_scr[slot, pl.ds(V7X_SUBLANES - 1, tm), :]
        r2 = s_scr[slot, pl.ds(V7X_SUBLANES - 2, tm), :]
        cw = fcw_ref[idx]
        y = fcb_ref[idx] + cw[2:3] * u
        y = y + cw[0:1] * r2
        return y + cw[1:2] * r1

    for c in range(nc):
        a = conv_chunk(c, 2 * (c % 2))
        bv = conv_chunk(c + nc, 2 * (c % 2) + 1)
        act_scr[:, c * width:(c + 1) * width] = (_gelu_tanh(a) * bv).astype(BF16)
    f = _dot(act_scr[...], wdn_ref[...])
    o_ref[0] = x + _rms(f, gpost_ref[...])


def _ffn_prompt(x, g_pre, g_post, w_up_c, fcw_c, fcb_c, w_dn, *, tm):
    b, t, d = x.shape
    nc2, _, cw = w_up_c.shape
    nc = nc2 // 2
    xblk = pl.BlockSpec((1, tm, d), lambda i, j: (i, j, 0))
    body = functools.partial(_ffn_prompt_body, tm=tm, nc=nc)
    return pl.pallas_call(
        body,
        grid=(b, t // tm),
        in_specs=[xblk, _const_spec((1, d)), _const_spec((1, d)), _const_spec(w_up_c.shape),
                  _const_spec(fcw_c.shape), _const_spec(fcb_c.shape), _const_spec(w_dn.shape)],
        out_specs=[xblk, pl.BlockSpec((1, nc2, V7X_SUBLANES, cw), lambda i, j: (i, 0, 0, 0))],
        out_shape=[jax.ShapeDtypeStruct((b, t, d), F32),
                   jax.ShapeDtypeStruct((b, nc2, V7X_SUBLANES, cw), F32)],
        scratch_shapes=[pltpu.VMEM((tm, d), BF16), pltpu.VMEM((tm, nc * cw), BF16),
                        pltpu.VMEM((nc2, V7X_SUBLANES, cw), F32),
                        pltpu.VMEM((4, tm + V7X_SUBLANES, cw), F32)],
        compiler_params=_params("arbitrary", "arbitrary"),
        name="ffn_prompt",
    )(x, g_pre, g_post, w_up_c, fcw_c, fcb_c, w_dn)


def _ffn_sample_body(x_ref, gpre_ref, gpost_ref, wup_ref, fcw_ref, fcb_ref, wdn_ref, st0_ref, st1_ref,
                     o_ref, up_ref, hn_scr, f_scr, *, nc):
    x = x_ref[...]
    hn_scr[...] = _rms(x, gpre_ref[...]).astype(BF16)
    f_scr[...] = jnp.zeros_like(f_scr)

    def conv_chunk(idx):
        u = _dot(hn_scr[...], wup_ref[idx])
        up_ref[idx] = u
        cw = fcw_ref[idx]
        y = fcb_ref[idx] + cw[2:3] * u
        y = y + cw[0:1] * st0_ref[idx]
        return y + cw[1:2] * st1_ref[idx]

    def chunk(c, carry):
        act = (_gelu_tanh(conv_chunk(c)) * conv_chunk(c + nc)).astype(BF16)
        f_scr[...] += _dot(act, wdn_ref[c])
        return carry

    lax.fori_loop(0, nc, chunk, 0)
    o_ref[...] = x + _rms(f_scr[...], gpost_ref[...])


def _ffn_sample(x, g_pre, g_post, w_up_c, fcw_c, fcb_c, w_dn_c, st0_c, st1_c):
    n, d = x.shape
    nc2, _, cw = w_up_c.shape
    body = functools.partial(_ffn_sample_body, nc=nc2 // 2)
    return pl.pallas_call(
        body,
        grid=(1,),
        in_specs=[_const_spec((n, d)), _const_spec((1, d)), _const_spec((1, d)), _const_spec(w_up_c.shape),
                  _const_spec(fcw_c.shape), _const_spec(fcb_c.shape), _const_spec(w_dn_c.shape),
                  _const_spec(st0_c.shape), _const_spec(st1_c.shape)],
        out_specs=[_const_spec((n, d)), _const_spec((nc2, n, cw))],
        out_shape=[jax.ShapeDtypeStruct((n, d), F32), jax.ShapeDtypeStruct((nc2, n, cw), F32)],
        scratch_shapes=[pltpu.VMEM((n, d), BF16), pltpu.VMEM((n, d), F32)],
        compiler_params=_params("arbitrary"),
        name="ffn_sample",
    )(x, g_pre, g_post, w_up_c, fcw_c, fcb_c, w_dn_c, st0_c, st1_c)


FFN_CHUNK = 2 * V7X_LANES
SAMPLE_PAGES_PER_STEP = 32
SAMPLE_MEM_SEQS_PER_STEP = 8
SAMPLE_HEAD_ROWS = 2 * V7X_SUBLANES


def _chunk_cols(a, cw):
    n = a.shape[-1]
    a = a.reshape(a.shape[:-1] + (n // cw, cw))
    return jnp.moveaxis(a, -2, 0)


def _unchunk_cols(a):
    nc, r, cw = a.shape
    return jnp.moveaxis(a, 0, 1).reshape(r, nc * cw)


def kernel(x_prompt, x_sample, cache_sb_k, cache_sb_v, cache_mem_k, cache_mem_v, state_conv, state_ffn_conv,
           page_table, mem_prompt, g_mix_pre, g_mix_post, g_ffn_pre, g_ffn_post, g_mem, w_in, conv_w, conv_b,
           sb_bias, w_mem_kv, w_branch, w_o, w_up, ffn_conv_w, ffn_conv_b, w_down):
    depth = w_in.shape[0]
    bp, seq, d = x_prompt.shape
    bs, dec_seq, _ = x_sample.shape
    assert dec_seq == 1, "the sample group advances one token per step"
    _, n_pool, page, sb_heads, sb_hd = cache_sb_k.shape
    _, _, n_mem, mem_heads, mem_hd = cache_mem_k.shape
    w = sb_heads * sb_hd
    d_ff = w_down.shape[1]
    assert mem_hd % V7X_LANES == 0 and V7X_LANES % sb_hd == 0 and d_ff % FFN_CHUNK == 0
    sb_scale = sb_hd ** -0.5
    assert math.frexp(sb_scale)[0] == 0.5, "the logit scale must be a power of two to fold into q exactly"

    tm_in = _pick_tile(seq, 512)
    tm_mix = _pick_tile(seq, 512)
    tm_ffn = _pick_tile(seq, 512)
    tq = _pick_tile(seq, 2 * V7X_LANES)
    tk = _pick_tile(tq, 2 * V7X_LANES)

    xp = x_prompt
    xs = x_sample.reshape(bs, d)
    conv_p, k_p, v_p, mk_p, mv_p, ffn_p = [], [], [], [], [], []
    conv_s, k_s, v_s, ffn_s = [], [], [], []
    for l in range(depth):
        row = lambda a: a[l].reshape(1, -1)
        w_a = w_in[l][:, :7 * w].astype(BF16)
        w_gate = jnp.moveaxis(w_in[l][:, 7 * w:].reshape(d, N_BRANCH, d), 1, 0).astype(BF16)
        w_br = w_branch[l].astype(BF16)
        w_out = w_o[l].astype(BF16)
        w_up_c = _chunk_cols(w_up[l], FFN_CHUNK).astype(BF16)
        w_dn_c = w_down[l].reshape(d_ff // FFN_CHUNK, FFN_CHUNK, d).astype(BF16)
        fcw_c = _chunk_cols(ffn_conv_w[l], FFN_CHUNK)
        fcb_c = _chunk_cols(ffn_conv_b[l].reshape(1, -1), FFN_CHUNK)
        bias = sb_bias[l]

        mk, mv, mkb, mvb = _memkv(mem_prompt, row(g_mem), w_mem_kv[l].astype(BF16))
        ya, q, k, v, kb, vb, yc, cs = _inproj_prompt(
            xp, row(g_mix_pre), w_a, conv_w[l], row(conv_b), mkb, mvb,
            sb_scale=sb_scale, mem_hd=mem_hd, tm=tm_in)
        yb = _sb_prompt(q, kb, vb, bias, hd=sb_hd, tq=tq, tk=tk)
        x1 = _mix(xp, ya, yb, yc, row(g_mix_pre), row(g_mix_post), w_gate, w_br, w_out, tm=tm_mix)
        xp, st = _ffn_prompt(x1, row(g_ffn_pre), row(g_ffn_post), w_up_c, fcw_c, fcb_c,
                             w_dn_c.reshape(d_ff, d), tm=tm_ffn)
        fs = jnp.moveaxis(st[:, :, V7X_SUBLANES - (CONV_WIDTH - 1):, :], 1, 2).reshape(bp, CONV_WIDTH - 1, -1)
        conv_p.append(cs)
        k_p.append(k.reshape(bp, seq, sb_heads, sb_hd))
        v_p.append(v.reshape(bp, seq, sb_heads, sb_hd))
        mk_p.append(mk.reshape(bp, n_mem, mem_heads, mem_hd))
        mv_p.append(mv.reshape(bp, n_mem, mem_heads, mem_hd))
        ffn_p.append(fs)

        st_c = state_conv[l]
        ya, q, k, v, mq, ccu = _inproj_sample(xs, row(g_mix_pre), w_a, conv_w[l], row(conv_b),
                                              st_c[:, 0], st_c[:, 1], sb_scale=sb_scale)
        bias_col = jnp.zeros((SAMPLE_HEAD_ROWS, 1), F32).at[:sb_heads, 0].set(bias)
        feature_major = lambda c: jnp.transpose(c[l], (0, 2, 3, 1)).reshape(n_pool, w, page)
        yb = _sb_sample(q, bias_col, feature_major(cache_sb_k), feature_major(cache_sb_v),
                        page_table, hd=sb_hd, pp=SAMPLE_PAGES_PER_STEP, rows=SAMPLE_HEAD_ROWS)
        yc = _mem_sample(mq.reshape(bs, mem_heads, mem_hd),
                         cache_mem_k[l].reshape(bs, n_mem * mem_heads, mem_hd),
                         cache_mem_v[l].reshape(bs, n_mem * mem_heads, mem_hd),
                         seqs=_pick_tile(bs, SAMPLE_MEM_SEQS_PER_STEP)).reshape(bs, w)
        x1 = _mix(xs[None], ya[None], yb[None], yc[None], row(g_mix_pre), row(g_mix_post),
                  w_gate, w_br, w_out, tm=bs)[0]
        st_f = state_ffn_conv[l]
        xs, up_c = _ffn_sample(x1, row(g_ffn_pre), row(g_ffn_post), w_up_c, fcw_c, fcb_c, w_dn_c,
                               _chunk_cols(st_f[:, 0], FFN_CHUNK), _chunk_cols(st_f[:, 1], FFN_CHUNK))
        conv_s.append(jnp.stack([st_c[:, 1], ccu], axis=1))
        k_s.append(k.reshape(bs, 1, sb_heads, sb_hd))
        v_s.append(v.reshape(bs, 1, sb_heads, sb_hd))
        ffn_s.append(jnp.stack([st_f[:, 1], _unchunk_cols(up_c)], axis=1))

    return (xp, xs.reshape(bs, 1, d), jnp.stack(conv_p), jnp.stack(k_p), jnp.stack(v_p), jnp.stack(mk_p),
            jnp.stack(mv_p), jnp.stack(ffn_p), jnp.stack(conv_s), jnp.stack(k_s), jnp.stack(v_s),
            jnp.stack(ffn_s))
```

```python
import functools
import math

import jax
import jax.numpy as jnp
from jax import lax
from jax.experimental import pallas as pl
from jax.experimental.pallas import tpu as pltpu

EPS = 1e-6
CONV_WIDTH = 3
N_BRANCH = 3
F32 = jnp.float32
BF16 = jnp.bfloat16

V7X_LANES = 128
V7X_SUBLANES = 8
V7X_VMEM_BYTES = 64 * 1024 * 1024
VMEM_LIMIT_BYTES = (V7X_VMEM_BYTES * 7) // 8


def _params(*semantics):
    return pltpu.CompilerParams(dimension_semantics=semantics, vmem_limit_bytes=VMEM_LIMIT_BYTES)


def _rms(x, g):
    ms = jnp.mean(x * x, axis=-1, keepdims=True)
    return (x * lax.rsqrt(ms + EPS)) * g


def _dot(a, b):
    return jnp.dot(a, b, preferred_element_type=F32)


def _dot_nt(a, b):
    return lax.dot_general(a, b, (((1,), (1,)), ((), ())), preferred_element_type=F32)


def _const_spec(shape):
    nd = len(shape)
    return pl.BlockSpec(shape, lambda *_: (0,) * nd, pipeline_mode=pl.Buffered(1))


def _pick_tile(n, want):
    t = min(n, want)
    assert n % t == 0, (n, t)
    return t


def _memkv_body(mem_ref, g_ref, w_ref, mk_ref, mv_ref, mkb_ref, mvb_ref):
    h = _rms(mem_ref[0], g_ref[...]).astype(BF16)
    kv = _dot(h, w_ref[...])
    w = kv.shape[1] // 2
    mk_ref[0] = kv[:, :w]
    mv_ref[0] = kv[:, w:]
    mkb_ref[0] = kv[:, :w].astype(BF16)
    mvb_ref[0] = kv[:, w:].astype(BF16)


def _memkv(mem, g_mem, w_mem_kv_bf):
    b, n_mem, d = mem.shape
    w = w_mem_kv_bf.shape[1] // 2
    blk = pl.BlockSpec((1, n_mem, w), lambda i: (i, 0, 0))
    return pl.pallas_call(
        _memkv_body,
        grid=(b,),
        in_specs=[pl.BlockSpec((1, n_mem, d), lambda i: (i, 0, 0)), _const_spec((1, d)),
                  _const_spec((d, 2 * w))],
        out_specs=[blk, blk, blk, blk],
        out_shape=[jax.ShapeDtypeStruct((b, n_mem, w), F32)] * 2
        + [jax.ShapeDtypeStruct((b, n_mem, w), BF16)] * 2,
        compiler_params=_params("arbitrary"),
        name="memkv",
    )(mem, g_mem, w_mem_kv_bf)


def _softmax_rows(s):
    m = jnp.max(s, axis=-1, keepdims=True)
    p = jnp.exp(s - m)
    return p / jnp.sum(p, axis=-1, keepdims=True)


def _inproj_prompt_body(x_ref, g_ref, w_ref, cw_ref, cb_ref, mk_ref, mv_ref,
                        ya_ref, q_ref, k_ref, v_ref, kb_ref, vb_ref, yc_ref, cs_ref,
                        s_scr, *, tm, w, sb_scale, mem_hd):
    ti = pl.program_id(1)
    h = _rms(x_ref[0], g_ref[...]).astype(BF16)

    def grp(i):
        return _dot(h, w_ref[:, i * w:(i + 1) * w])

    ccu = grp(1) * grp(2)
    @pl.when(ti == 0)
    def _():
        s_scr[tm:tm + V7X_SUBLANES, :] = jnp.zeros((V7X_SUBLANES, w), F32)

    s_scr[0:V7X_SUBLANES, :] = s_scr[tm:tm + V7X_SUBLANES, :]
    s_scr[V7X_SUBLANES:, :] = ccu
    r1 = s_scr[pl.ds(V7X_SUBLANES - 1, tm), :]
    r2 = s_scr[pl.ds(V7X_SUBLANES - 2, tm), :]
    cw = cw_ref[...]
    conv = cb_ref[...] + cw[2:3] * ccu
    conv = conv + cw[0:1] * r2
    conv = conv + cw[1:2] * r1
    ya_ref[0] = (grp(0) * conv).astype(BF16)
    cs_ref[0] = s_scr[pl.ds(tm + V7X_SUBLANES - (CONV_WIDTH - 1), CONV_WIDTH - 1), :]

    q_ref[0] = (grp(3) * sb_scale).astype(BF16)
    k = grp(4)
    k_ref[0] = k
    kb_ref[0] = k.astype(BF16)
    v = grp(5)
    v_ref[0] = v
    vb_ref[0] = v.astype(BF16)

    mq = grp(6)
    outs = []
    for hd in range(w // mem_hd):
        sl = slice(hd * mem_hd, (hd + 1) * mem_hd)
        s = _dot_nt(mq[:, sl].astype(BF16), mk_ref[0, :, sl]) * (mem_hd ** -0.5)
        p = _softmax_rows(s)
        outs.append(_dot(p.astype(BF16), mv_ref[0, :, sl]))
    yc_ref[0] = jnp.concatenate(outs, axis=1).astype(BF16)


def _inproj_prompt(x, g, w_a, conv_w, conv_b, mkb, mvb, *, sb_scale, mem_hd, tm):
    b, t, d = x.shape
    w = conv_w.shape[1]
    n_mem = mkb.shape[1]
    row = lambda dt: jax.ShapeDtypeStruct((b, t, w), dt)
    blk = pl.BlockSpec((1, tm, w), lambda i, j: (i, j, 0))
    body = functools.partial(_inproj_prompt_body, tm=tm, w=w, sb_scale=sb_scale, mem_hd=mem_hd)
    return pl.pallas_call(
        body,
        grid=(b, t // tm),
        in_specs=[pl.BlockSpec((1, tm, d), lambda i, j: (i, j, 0)), _const_spec((1, d)),
                  _const_spec(w_a.shape), _const_spec(conv_w.shape), _const_spec((1, w)),
                  pl.BlockSpec((1, n_mem, w), lambda i, j: (i, 0, 0)),
                  pl.BlockSpec((1, n_mem, w), lambda i, j: (i, 0, 0))],
        out_specs=[blk] * 7 + [pl.BlockSpec((1, CONV_WIDTH - 1, w), lambda i, j: (i, 0, 0))],
        out_shape=[row(BF16), row(BF16), row(F32), row(F32), row(BF16), row(BF16), row(BF16),
                   jax.ShapeDtypeStruct((b, CONV_WIDTH - 1, w), F32)],
        scratch_shapes=[pltpu.VMEM((tm + V7X_SUBLANES, w), F32)],
        compiler_params=_params("arbitrary", "arbitrary"),
        name="inproj_prompt",
    )(x, g, w_a, conv_w, conv_b, mkb, mvb)


def _inproj_sample_body(x_ref, g_ref, w_ref, cw_ref, cb_ref, st0_ref, st1_ref,
                        ya_ref, q_ref, k_ref, v_ref, mq_ref, ccu_ref, *, w, sb_scale):
    h = _rms(x_ref[...], g_ref[...]).astype(BF16)

    def grp(i):
        return _dot(h, w_ref[:, i * w:(i + 1) * w])

    ccu = grp(1) * grp(2)
    cw = cw_ref[...]
    conv = cb_ref[...] + cw[2:3] * ccu
    conv = conv + cw[0:1] * st0_ref[...]
    conv = conv + cw[1:2] * st1_ref[...]
    ya_ref[...] = (grp(0) * conv).astype(BF16)
    ccu_ref[...] = ccu
    q_ref[...] = grp(3) * sb_scale
    k_ref[...] = grp(4)
    v_ref[...] = grp(5)
    mq_ref[...] = grp(6)


def _inproj_sample(x, g, w_a, conv_w, conv_b, st0, st1, *, sb_scale):
    n, d = x.shape
    w = conv_w.shape[1]
    row = lambda dt: jax.ShapeDtypeStruct((n, w), dt)
    body = functools.partial(_inproj_sample_body, w=w, sb_scale=sb_scale)
    return pl.pallas_call(
        body,
        grid=(1,),
        in_specs=[_const_spec((n, d)), _const_spec((1, d)), _const_spec(w_a.shape),
                  _const_spec(conv_w.shape), _const_spec((1, w)), _const_spec((n, w)),
                  _const_spec((n, w))],
        out_specs=[_const_spec((n, w))] * 6,
        out_shape=[row(BF16), row(F32), row(F32), row(F32), row(F32), row(F32)],
        compiler_params=_params("arbitrary"),
        name="inproj_sample",
    )(x, g, w_a, conv_w, conv_b, st0, st1)


_SIGN_BIT = -2 ** 31


def _softplus(z):
    neg_abs = lax.bitcast_convert_type(lax.bitcast_convert_type(z, jnp.int32) | _SIGN_BIT, F32)
    return jnp.maximum(z, 0.0) + jnp.log(1.0 + jnp.exp(neg_abs))


def _sb_log_weights(z, causal, u):
    sp = _softplus(z)
    if causal is not None:
        sp = jnp.where(causal, sp, 0.0)
    after = _dot(sp.astype(BF16), u)
    total = after[:, :1] + sp[:, :1]
    return (z - sp) - after, jnp.broadcast_to(total, (z.shape[0], V7X_LANES))


def _cumsum_weights(n):
    row = lax.broadcasted_iota(jnp.int32, (n, n), 0)
    col = lax.broadcasted_iota(jnp.int32, (n, n), 1)
    return jnp.where(row > col, 1.0, 0.0).astype(BF16)


def _lanes(x, n):
    return jnp.concatenate([x] * (n // V7X_LANES), axis=1)


def _sb_prompt_body(bias_ref, q_ref, k_ref, v_ref, o_ref, qs_scr, acc_scr, carry_scr,
                    lwa_scr, lwb_scr, resta_scr, restb_scr, *, tq, tk, hd):
    hp = pl.program_id(1)
    i = pl.program_id(2)
    hpb = V7X_LANES // hd
    ratio = tq // tk
    q2 = q_ref[0]
    lane_head = lax.broadcasted_iota(jnp.int32, (tq, V7X_LANES), 1) // hd
    for h in range(hpb):
        qs_scr[h * tq:(h + 1) * tq, :] = jnp.where(lane_head == h, q2, jnp.zeros_like(q2))
    u = _cumsum_weights(tk)
    acc_scr[...] = jnp.zeros_like(acc_scr)
    carry_scr[...] = jnp.zeros_like(carry_scr)

    def log_weights(j, mask):
        ks = pl.multiple_of(j * tk, tk)
        mm = _dot_nt(qs_scr[...], k_ref[0, pl.ds(ks, tk), :])
        z = jnp.concatenate([mm[h * tq:(h + 1) * tq] + bias_ref[hp * hpb + h] for h in range(hpb)], axis=0)
        return _sb_log_weights(z, mask, u)

    def values(j):
        return v_ref[0, pl.ds(pl.multiple_of(j * tk, tk), tk), :]

    def chain(j_top, masks):
        n = len(masks)
        parts = [log_weights(j_top - t, masks[t]) for t in range(n)]
        carry = carry_scr[...]
        weights = []
        for t, (lw, rest) in enumerate(parts):
            a = jnp.exp(lw - _lanes(carry, tk))
            if masks[t] is not None:
                a = jnp.where(masks[t], a, 0.0)
            weights.append(a.astype(BF16))
            carry = carry + rest
        acc_scr[...] += _dot(jnp.concatenate(weights, axis=1),
                             jnp.concatenate([values(j_top - t) for t in range(n)], axis=0))
        carry_scr[...] = carry

    row = lax.broadcasted_iota(jnp.int32, (tq, tk), 0)
    col = lax.broadcasted_iota(jnp.int32, (tq, tk), 1)
    diag_masks = [jnp.concatenate([col + d * tk < row] * hpb, axis=0) for d in range(ratio - 1, -1, -1)]
    n_full = ratio * i
    odd = lax.bitwise_and(n_full, 1)
    n_pairs = lax.shift_right_logical(n_full, 1)
    top = n_full - 1 - odd

    def front(p, lw_ref, rest_ref):
        for t in range(2):
            j = jnp.maximum(top - 2 * p - t, 0)
            lw, rest = log_weights(j, None)
            lw_ref[t] = lw
            rest_ref[t] = rest

    def back(p, lw_ref, rest_ref):
        j = top - 2 * p
        carry = carry_scr[...]
        rest1 = rest_ref[0]
        a1 = jnp.exp(lw_ref[0] - _lanes(carry, tk)).astype(BF16)
        a0 = jnp.exp(lw_ref[1] - _lanes(carry + rest1, tk)).astype(BF16)
        acc_scr[...] += _dot(jnp.concatenate([a1, a0], axis=1),
                             jnp.concatenate([values(j), values(j - 1)], axis=0))
        carry_scr[...] = carry + rest1 + rest_ref[1]

    @pl.when(odd == 0)
    def _():
        front(0, lwa_scr, resta_scr)
        chain(ratio * i + ratio - 1, diag_masks)

    @pl.when(odd == 1)
    def _():
        front(0, lwa_scr, resta_scr)
        chain(ratio * i + ratio - 1, diag_masks + [None])

    def loop_body(n, c):
        p = 2 * n
        front(p + 1, lwb_scr, restb_scr)
        back(p, lwa_scr, resta_scr)

        @pl.when(p + 1 < n_pairs)
        def _():
            front(p + 2, lwa_scr, resta_scr)
            back(p + 1, lwb_scr, restb_scr)

        return c

    lax.fori_loop(0, lax.shift_right_logical(n_pairs + 1, 1), loop_body, 0)

    out = acc_scr[(hpb - 1) * tq:, :]
    for h in range(hpb - 2, -1, -1):
        out = jnp.where(lane_head == h, acc_scr[h * tq:(h + 1) * tq, :], out)
    o_ref[0] = out.astype(BF16)


def _sb_prompt(q, k, v, bias, *, hd, tq, tk):
    b, t, w = q.shape
    assert tq % tk == 0 and t % tq == 0
    body = functools.partial(_sb_prompt_body, tq=tq, tk=tk, hd=hd)
    hpb = V7X_LANES // hd
    pair = (2, hpb * tq)
    grid_spec = pltpu.PrefetchScalarGridSpec(
        num_scalar_prefetch=1,
        grid=(b, w // V7X_LANES, t // tq),
        in_specs=[pl.BlockSpec((1, tq, V7X_LANES), lambda bi, hp, i, _: (bi, i, hp)),
                  pl.BlockSpec((1, t, V7X_LANES), lambda bi, hp, i, _: (bi, 0, hp)),
                  pl.BlockSpec((1, t, V7X_LANES), lambda bi, hp, i, _: (bi, 0, hp))],
        out_specs=pl.BlockSpec((1, tq, V7X_LANES), lambda bi, hp, i, _: (bi, i, hp)),
        scratch_shapes=[pltpu.VMEM((hpb * tq, V7X_LANES), BF16), pltpu.VMEM((hpb * tq, V7X_LANES), F32),
                        pltpu.VMEM((hpb * tq, V7X_LANES), F32),
                        pltpu.VMEM(pair + (tk,), F32), pltpu.VMEM(pair + (tk,), F32),
                        pltpu.VMEM(pair + (V7X_LANES,), F32), pltpu.VMEM(pair + (V7X_LANES,), F32)],
    )
    return pl.pallas_call(
        body,
        grid_spec=grid_spec,
        out_shape=jax.ShapeDtypeStruct((b, t, w), BF16),
        compiler_params=_params("arbitrary", "arbitrary", "arbitrary"),
        name="sb_prompt",
    )(bias, q, k, v)


def _sb_sample_body(pt_ref, q_ref, bias_ref, *refs, pp, hd, rows):
    k_refs = refs[:pp]
    v_refs = refs[pp:2 * pp]
    o_ref = refs[2 * pp]
    acc_scr, carry_scr = refs[2 * pp + 1:]
    g = pl.program_id(1)
    w = q_ref.shape[-1]
    page = k_refs[0].shape[2]

    @pl.when(g == 0)
    def _():
        acc_scr[...] = jnp.zeros_like(acc_scr)
        carry_scr[...] = jnp.zeros_like(carry_scr)

    row_head = lax.broadcasted_iota(jnp.int32, (rows, w), 0)
    lane_head = lax.broadcasted_iota(jnp.int32, (rows, w), 1) // hd
    on_diag = row_head == lane_head
    qrow = jnp.broadcast_to(q_ref[0], (rows, w))
    qbd = jnp.where(on_diag, qrow, 0.0).astype(BF16)
    u = _cumsum_weights(page)

    kt = jnp.concatenate([k_refs[r][0] for r in range(pp)], axis=1).astype(BF16)
    z = _dot(qbd, kt) + bias_ref[...]
    z_st = jnp.concatenate([z[:, r * page:(r + 1) * page] for r in range(pp)], axis=0)
    lw_st, rest_st = _sb_log_weights(z_st, None, u)
    after = carry_scr[...]
    afters = [None] * pp
    for r in range(pp - 1, -1, -1):
        afters[r] = after
        after = after + rest_st[r * rows:(r + 1) * rows]
    carry_scr[...] = after
    a_st = jnp.exp(lw_st - _lanes(jnp.concatenate(afters, axis=0), page)).astype(BF16)
    a = jnp.concatenate([a_st[r * rows:(r + 1) * rows] for r in range(pp)], axis=1)
    vt = jnp.concatenate([v_refs[r][0] for r in range(pp)], axis=1).astype(BF16)
    acc_scr[...] += _dot_nt(a, vt)

    @pl.when(g == pl.num_programs(1) - 1)
    def _():
        picked = jnp.where(on_diag, acc_scr[...], 0.0)
        o_ref[0] = jnp.sum(picked, axis=0, keepdims=True).astype(BF16)


def _sb_sample(q, bias_col, cache_kt, cache_vt, page_table, *, hd, pp, rows):
    n, w = q.shape
    n_pages = page_table.shape[1]
    page = cache_kt.shape[2]
    assert n_pages % pp == 0
    ng = n_pages // pp

    def page_spec(r):
        return pl.BlockSpec((1, w, page), lambda bi, g, pt: (pt[bi, (ng - 1 - g) * pp + r], 0, 0))

    body = functools.partial(_sb_sample_body, pp=pp, hd=hd, rows=rows)
    grid_spec = pltpu.PrefetchScalarGridSpec(
        num_scalar_prefetch=1,
        grid=(n, ng),
        in_specs=[pl.BlockSpec((1, 1, w), lambda bi, g, pt: (bi, 0, 0)),
                  pl.BlockSpec((rows, 1), lambda bi, g, pt: (0, 0))]
        + [page_spec(r) for r in range(pp)] * 2,
        out_specs=pl.BlockSpec((1, 1, w), lambda bi, g, pt: (bi, 0, 0)),
        scratch_shapes=[pltpu.VMEM((rows, w), F32), pltpu.VMEM((rows, V7X_LANES), F32)],
    )
    out = pl.pallas_call(
        body,
        grid_spec=grid_spec,
        out_shape=jax.ShapeDtypeStruct((n, 1, w), BF16),
        compiler_params=_params("arbitrary", "arbitrary"),
        name="sb_sample",
    )(page_table, q.reshape(n, 1, w), bias_col, *([cache_kt] * pp), *([cache_vt] * pp))
    return out.reshape(n, w)


def _mem_sample_body(q_ref, mk_ref, mv_ref, o_ref, *, heads):
    g = V7X_SUBLANES
    seqs, rows, hd = mk_ref.shape

    def over_tokens(x, op, reduce):
        y = reduce(x, axis=0)
        out = y
        for rep in range(1, g // heads):
            out = op(out, pltpu.roll(y, rep * heads, axis=0))
        return out

    def one_sequence(b, carry):
        k3 = mk_ref[b].reshape(rows // g, g, hd)
        v3 = mv_ref[b].reshape(rows // g, g, hd)
        s = jnp.sum(k3 * q_ref[b], axis=-1, keepdims=True) * (hd ** -0.5)
        s = jnp.broadcast_to(s, k3.shape)
        p = jnp.exp(s - over_tokens(s, jnp.maximum, jnp.max))
        p = p / over_tokens(p, jnp.add, jnp.sum)
        o_ref[b] = over_tokens(p * v3, jnp.add, jnp.sum)[:heads].astype(BF16)
        return carry

    lax.fori_loop(0, seqs, one_sequence, 0)


def _mem_sample(mq, mem_k, mem_v, *, seqs):
    n, heads, hd = mq.shape
    rows = mem_k.shape[1]
    assert V7X_SUBLANES % heads == 0 and n % seqs == 0
    q8 = jnp.tile(mq, (1, V7X_SUBLANES // heads, 1))
    mem = pl.BlockSpec((seqs, rows, hd), lambda i: (i, 0, 0))
    return pl.pallas_call(
        functools.partial(_mem_sample_body, heads=heads),
        grid=(n // seqs,),
        in_specs=[pl.BlockSpec((seqs, V7X_SUBLANES, hd), lambda i: (i, 0, 0)), mem, mem],
        out_specs=pl.BlockSpec((seqs, heads, hd), lambda i: (i, 0, 0)),
        out_shape=jax.ShapeDtypeStruct((n, heads, hd), BF16),
        compiler_params=_params("arbitrary"),
        name="mem_sample",
    )(q8, mem_k, mem_v)


def _mix_body(x_ref, ya_ref, yb_ref, yc_ref, gpre_ref, gpost_ref, wg_ref, wb_ref, wo_ref, o_ref):
    x = x_ref[0]
    h = _rms(x, gpre_ref[...]).astype(BF16)
    ys = (ya_ref[0], yb_ref[0], yc_ref[0])
    merged = None
    for n in range(N_BRANCH):
        gate = 1.0 / (1.0 + jnp.exp(-_dot(h, wg_ref[n])))
        term = gate * _dot(ys[n], wb_ref[n])
        merged = term if merged is None else merged + term
    mixed = _dot(merged.astype(BF16), wo_ref[...])
    o_ref[0] = x + _rms(mixed, gpost_ref[...])


def _mix(x, ya, yb, yc, g_pre, g_post, w_gate, w_branch, w_o, *, tm):
    b, t, d = x.shape
    w = ya.shape[-1]
    xblk = pl.BlockSpec((1, tm, d), lambda i, j: (i, j, 0))
    yblk = pl.BlockSpec((1, tm, w), lambda i, j: (i, j, 0))
    return pl.pallas_call(
        _mix_body,
        grid=(b, t // tm),
        in_specs=[xblk, yblk, yblk, yblk, _const_spec((1, d)), _const_spec((1, d)),
                  _const_spec(w_gate.shape), _const_spec(w_branch.shape), _const_spec(w_o.shape)],
        out_specs=xblk,
        out_shape=jax.ShapeDtypeStruct((b, t, d), F32),
        compiler_params=_params("arbitrary", "arbitrary"),
        name="mix",
    )(x, ya, yb, yc, g_pre, g_post, w_gate, w_branch, w_o)


def _gelu_tanh(x):
    return x * (0.5 * (1.0 + jnp.tanh((2.0 / jnp.pi) ** 0.5 * (x + 0.044715 * (x * x * x)))))


def _ffn_prompt_body(x_ref, gpre_ref, gpost_ref, wup_ref, fcw_ref, fcb_ref, wdn_ref,
                     o_ref, st_ref, hn_scr, act_scr, carry_scr, s_scr, *, tm, nc):
    ti = pl.program_id(1)
    width = wup_ref.shape[2]

    @pl.when(ti == 0)
    def _():
        carry_scr[...] = jnp.zeros_like(carry_scr)

    x = x_ref[0]
    hn_scr[...] = _rms(x, gpre_ref[...]).astype(BF16)

    def conv_chunk(idx, slot):
        u = _dot(hn_scr[...], wup_ref[idx])
        s_scr[slot, 0:V7X_SUBLANES, :] = carry_scr[idx]
        s_scr[slot, V7X_SUBLANES:, :] = u
        last = u[tm - V7X_SUBLANES:, :]
        carry_scr[idx] = last
        st_ref[0, idx] = last
        r1 = s_scr[slot, pl.ds(V7X_SUBLANES - 1, tm), :]
        r2 = s_scr[slot, pl.ds(V7X_SUBLANES - 2, tm), :]
        cw = fcw_ref[idx]
        y = fcb_ref[idx] + cw[2:3] * u
        y = y + cw[0:1] * r2
        return y + cw[1:2] * r1

    for c in range(nc):
        a = conv_chunk(c, 2 * (c % 2))
        bv = conv_chunk(c + nc, 2 * (c % 2) + 1)
        act_scr[:, c * width:(c + 1) * width] = (_gelu_tanh(a) * bv).astype(BF16)
    f = _dot(act_scr[...], wdn_ref[...])
    o_ref[0] = x + _rms(f, gpost_ref[...])


def _ffn_prompt(x, g_pre, g_post, w_up_c, fcw_c, fcb_c, w_dn, *, tm):
    b, t, d = x.shape
    nc2, _, cw = w_up_c.shape
    nc = nc2 // 2
    xblk = pl.BlockSpec((1, tm, d), lambda i, j: (i, j, 0))
    body = functools.partial(_ffn_prompt_body, tm=tm, nc=nc)
    return pl.pallas_call(
        body,
        grid=(b, t // tm),
        in_specs=[xblk, _const_spec((1, d)), _const_spec((1, d)), _const_spec(w_up_c.shape),
                  _const_spec(fcw_c.shape), _const_spec(fcb_c.shape), _const_spec(w_dn.shape)],
        out_specs=[xblk, pl.BlockSpec((1, nc2, V7X_SUBLANES, cw), lambda i, j: (i, 0, 0, 0))],
        out_shape=[jax.ShapeDtypeStruct((b, t, d), F32),
                   jax.ShapeDtypeStruct((b, nc2, V7X_SUBLANES, cw), F32)],
        scratch_shapes=[pltpu.VMEM((tm, d), BF16), pltpu.VMEM((tm, nc * cw), BF16),
                        pltpu.VMEM((nc2, V7X_SUBLANES, cw), F32),
                        pltpu.VMEM((4, tm + V7X_SUBLANES, cw), F32)],
        compiler_params=_params("arbitrary", "arbitrary"),
        name="ffn_prompt",
    )(x, g_pre, g_post, w_up_c, fcw_c, fcb_c, w_dn)


def _ffn_sample_body(x_ref, gpre_ref, gpost_ref, wup_ref, fcw_ref, fcb_ref, wdn_ref, st0_ref, st1_ref,
                     o_ref, up_ref, hn_scr, f_scr, *, nc):
    x = x_ref[...]
    hn_scr[...] = _rms(x, gpre_ref[...]).astype(BF16)
    f_scr[...] = jnp.zeros_like(f_scr)

    def conv_chunk(idx):
        u = _dot(hn_scr[...], wup_ref[idx])
        up_ref[idx] = u
        cw = fcw_ref[idx]
        y = fcb_ref[idx] + cw[2:3] * u
        y = y + cw[0:1] * st0_ref[idx]
        return y + cw[1:2] * st1_ref[idx]

    def chunk(c, carry):
        act = (_gelu_tanh(conv_chunk(c)) * conv_chunk(c + nc)).astype(BF16)
        f_scr[...] += _dot(act, wdn_ref[c])
        return carry

    lax.fori_loop(0, nc, chunk, 0)
    o_ref[...] = x + _rms(f_scr[...], gpost_ref[...])


def _ffn_sample(x, g_pre, g_post, w_up_c, fcw_c, fcb_c, w_dn_c, st0_c, st1_c):
    n, d = x.shape
    nc2, _, cw = w_up_c.shape
    body = functools.partial(_ffn_sample_body, nc=nc2 // 2)
    return pl.pallas_call(
        body,
        grid=(1,),
        in_specs=[_const_spec((n, d)), _const_spec((1, d)), _const_spec((1, d)), _const_spec(w_up_c.shape),
                  _const_spec(fcw_c.shape), _const_spec(fcb_c.shape), _const_spec(w_dn_c.shape),
                  _const_spec(st0_c.shape), _const_spec(st1_c.shape)],
        out_specs=[_const_spec((n, d)), _const_spec((nc2, n, cw))],
        out_shape=[jax.ShapeDtypeStruct((n, d), F32), jax.ShapeDtypeStruct((nc2, n, cw), F32)],
        scratch_shapes=[pltpu.VMEM((n, d), BF16), pltpu.VMEM((n, d), F32)],
        compiler_params=_params("arbitrary"),
        name="ffn_sample",
    )(x, g_pre, g_post, w_up_c, fcw_c, fcb_c, w_dn_c, st0_c, st1_c)


FFN_CHUNK = 2 * V7X_LANES
SAMPLE_PAGES_PER_STEP = 32
SAMPLE_MEM_SEQS_PER_STEP = 8
SAMPLE_HEAD_ROWS = 2 * V7X_SUBLANES


def _chunk_cols(a, cw):
    n = a.shape[-1]
    a = a.reshape(a.shape[:-1] + (n // cw, cw))
    return jnp.moveaxis(a, -2, 0)


def _unchunk_cols(a):
    nc, r, cw = a.shape
    return jnp.moveaxis(a, 0, 1).reshape(r, nc * cw)


def kernel(x_prompt, x_sample, cache_sb_k, cache_sb_v, cache_mem_k, cache_mem_v, state_conv, state_ffn_conv,
           page_table, mem_prompt, g_mix_pre, g_mix_post, g_ffn_pre, g_ffn_post, g_mem, w_in, conv_w, conv_b,
           sb_bias, w_mem_kv, w_branch, w_o, w_up, ffn_conv_w, ffn_conv_b, w_down):
    depth = w_in.shape[0]
    bp, seq, d = x_prompt.shape
    bs, dec_seq, _ = x_sample.shape
    assert dec_seq == 1, "the sample group advances one token per step"
    _, n_pool, page, sb_heads, sb_hd = cache_sb_k.shape
    _, _, n_mem, mem_heads, mem_hd = cache_mem_k.shape
    w = sb_heads * sb_hd
    d_ff = w_down.shape[1]
    assert mem_hd % V7X_LANES == 0 and V7X_LANES % sb_hd == 0 and d_ff % FFN_CHUNK == 0
    sb_scale = sb_hd ** -0.5
    assert math.frexp(sb_scale)[0] == 0.5, "the logit scale must be a power of two to fold into q exactly"

    tm_in = _pick_tile(seq, 512)
    tm_mix = _pick_tile(seq, 512)
    tm_ffn = _pick_tile(seq, 512)
    tq = _pick_tile(seq, 2 * V7X_LANES)
    tk = _pick_tile(tq, 2 * V7X_LANES)

    xp = x_prompt
    xs = x_sample.reshape(bs, d)
    conv_p, k_p, v_p, mk_p, mv_p, ffn_p = [], [], [], [], [], []
    conv_s, k_s, v_s, ffn_s = [], [], [], []
    for l in range(depth):
        row = lambda a: a[l].reshape(1, -1)
        w_a = w_in[l][:, :7 * w].astype(BF16)
        w_gate = jnp.moveaxis(w_in[l][:, 7 * w:].reshape(d, N_BRANCH, d), 1, 0).astype(BF16)
        w_br = w_branch[l].astype(BF16)
        w_out = w_o[l].astype(BF16)
        w_up_c = _chunk_cols(w_up[l], FFN_CHUNK).astype(BF16)
        w_dn_c = w_down[l].reshape(d_ff // FFN_CHUNK, FFN_CHUNK, d).astype(BF16)
        fcw_c = _chunk_cols(ffn_conv_w[l], FFN_CHUNK)
        fcb_c = _chunk_cols(ffn_conv_b[l].reshape(1, -1), FFN_CHUNK)
        bias = sb_bias[l]

        mk, mv, mkb, mvb = _memkv(mem_prompt, row(g_mem), w_mem_kv[l].astype(BF16))
        ya, q, k, v, kb, vb, yc, cs = _inproj_prompt(
            xp, row(g_mix_pre), w_a, conv_w[l], row(conv_b), mkb, mvb,
            sb_scale=sb_scale, mem_hd=mem_hd, tm=tm_in)
        yb = _sb_prompt(q, kb, vb, bias, hd=sb_hd, tq=tq, tk=tk)
        x1 = _mix(xp, ya, yb, yc, row(g_mix_pre), row(g_mix_post), w_gate, w_br, w_out, tm=tm_mix)
        xp, st = _ffn_prompt(x1, row(g_ffn_pre), row(g_ffn_post), w_up_c, fcw_c, fcb_c,
                             w_dn_c.reshape(d_ff, d), tm=tm_ffn)
        fs = jnp.moveaxis(st[:, :, V7X_SUBLANES - (CONV_WIDTH - 1):, :], 1, 2).reshape(bp, CONV_WIDTH - 1, -1)
        conv_p.append(cs)
        k_p.append(k.reshape(bp, seq, sb_heads, sb_hd))
        v_p.append(v.reshape(bp, seq, sb_heads, sb_hd))
        mk_p.append(mk.reshape(bp, n_mem, mem_heads, mem_hd))
        mv_p.append(mv.reshape(bp, n_mem, mem_heads, mem_hd))
        ffn_p.append(fs)

        st_c = state_conv[l]
        ya, q, k, v, mq, ccu = _inproj_sample(xs, row(g_mix_pre), w_a, conv_w[l], row(conv_b),
                                              st_c[:, 0], st_c[:, 1], sb_scale=sb_scale)
        bias_col = jnp.zeros((SAMPLE_HEAD_ROWS, 1), F32).at[:sb_heads, 0].set(bias)
        feature_major = lambda c: jnp.transpose(c[l], (0, 2, 3, 1)).reshape(n_pool, w, page)
        yb = _sb_sample(q, bias_col, feature_major(cache_sb_k), feature_major(cache_sb_v),
                        page_table, hd=sb_hd, pp=SAMPLE_PAGES_PER_STEP, rows=SAMPLE_HEAD_ROWS)
        yc = _mem_sample(mq.reshape(bs, mem_heads, mem_hd),
                         cache_mem_k[l].reshape(bs, n_mem * mem_heads, mem_hd),
                         cache_mem_v[l].reshape(bs, n_mem * mem_heads, mem_hd),
                         seqs=_pick_tile(bs, SAMPLE_MEM_SEQS_PER_STEP)).reshape(bs, w)
        x1 = _mix(xs[None], ya[None], yb[None], yc[None], row(g_mix_pre), row(g_mix_post),
                  w_gate, w_br, w_out, tm=bs)[0]
        st_f = state_ffn_conv[l]
        xs, up_c = _ffn_sample(x1, row(g_ffn_pre), row(g_ffn_post), w_up_c, fcw_c, fcb_c, w_dn_c,
                               _chunk_cols(st_f[:, 0], FFN_CHUNK), _chunk_cols(st_f[:, 1], FFN_CHUNK))
        conv_s.append(jnp.stack([st_c[:, 1], ccu], axis=1))
        k_s.append(k.reshape(bs, 1, sb_heads, sb_hd))
        v_s.append(v.reshape(bs, 1, sb_heads, sb_hd))
        ffn_s.append(jnp.stack([st_f[:, 1], _unchunk_cols(up_c)], axis=1))

    return (xp, xs.reshape(bs, 1, d), jnp.stack(conv_p), jnp.stack(k_p), jnp.stack(v_p), jnp.stack(mk_p),
            jnp.stack(mv_p), jnp.stack(ffn_p), jnp.stack(conv_s), jnp.stack(k_s), jnp.stack(v_s),
            jnp.stack(ffn_s))
```
